```python
import math
import jax, jax.numpy as jnp
from jax import lax
import numpy as np

D_MODEL = 2048
BATCH = 4
SEQ = 4096
DEPTH = 2

N_A_LAYERS = DEPTH // 2
N_B_LAYERS = DEPTH - N_A_LAYERS
HEAD_DIM = 128
MEM_LEN = 256
MEM_HEADS = 4
MEM_WIDTH = MEM_HEADS * HEAD_DIM
MIX_WIDTH = D_MODEL
TOK_WIDTH = MIX_WIDTH - MEM_WIDTH
CHUNK = 128
SGU_GROUPS = TOK_WIDTH // HEAD_DIM
MLA_HEADS = TOK_WIDTH // HEAD_DIM
QK_NOPE = 128
QK_ROPE = 64
V_DIM = 128
Q_LORA = 512
KV_LORA = 512
ROPE_THETA = 10000.0
D_FF = ((8 * D_MODEL + 3 * 256 - 1) // (3 * 256)) * 256
Q_BLOCK = 128
EPS = 1e-6
MLA_SCALE = (QK_NOPE + QK_ROPE) ** -0.5
MEM_SCALE = HEAD_DIM ** -0.5
A_IN_WIDTH = 2 * TOK_WIDTH + MEM_WIDTH
B_IN_WIDTH = Q_LORA + MEM_WIDTH

kernel_name = "yoco_gmlp_mla_memory_hybrid"


def rms_norm(x, g):
    xf = x.astype(jnp.float32)
    y = xf * lax.rsqrt(jnp.mean(xf * xf, axis=-1, keepdims=True) + EPS)
    return (y * g.astype(jnp.float32)).astype(x.dtype)


def layer_norm(x, g, b):
    xf = x.astype(jnp.float32)
    mu = jnp.mean(xf, axis=-1, keepdims=True)
    xc = xf - mu
    y = xc * lax.rsqrt(jnp.mean(xc * xc, axis=-1, keepdims=True) + EPS)
    return (y * g.astype(jnp.float32) + b.astype(jnp.float32)).astype(x.dtype)


def rope_tables(positions):
    inv_freq = ROPE_THETA ** (-jnp.arange(0, QK_ROPE, 2, dtype=jnp.float32) / QK_ROPE)
    ang = positions.astype(jnp.float32)[..., None] * inv_freq
    return jnp.cos(ang), jnp.sin(ang)


def apply_rope(x, cos, sin):
    x1, x2 = jnp.split(x.astype(jnp.float32), 2, axis=-1)
    out = jnp.concatenate([x1 * cos - x2 * sin, x2 * cos + x1 * sin], axis=-1)
    return out.astype(x.dtype)


def swiglu_ffn(h, w_gate, w_up, w_down):
    return (jax.nn.silu(h @ w_gate) * (h @ w_up)) @ w_down


def memory_attention(q, mem, mem_g, w_mem_kv):
    B, S, _ = q.shape
    kv = rms_norm(mem, mem_g) @ w_mem_kv
    k, v = jnp.split(kv, 2, axis=-1)
    k = k.reshape(B, -1, MEM_HEADS, HEAD_DIM)
    v = v.reshape(B, -1, MEM_HEADS, HEAD_DIM)
    q = q.reshape(B, S, MEM_HEADS, HEAD_DIM)
    s = jnp.einsum('bqhd,bmhd->bhqm', q, k).astype(jnp.float32) * MEM_SCALE
    p = jax.nn.softmax(s, axis=-1).astype(v.dtype)
    return jnp.einsum('bhqm,bmhd->bqhd', p, v).reshape(B, S, MEM_WIDTH)


def gmlp_spatial_gating(z, ln_g, ln_b, w_s, b_s):
    B, S, _ = z.shape
    z = jax.nn.gelu(z)
    u, v = jnp.split(z, 2, axis=-1)
    v = layer_norm(v, ln_g, ln_b)
    v = v.reshape(B, S // CHUNK, CHUNK, SGU_GROUPS, HEAD_DIM)
    causal = jnp.tril(jnp.ones((CHUNK, CHUNK), dtype=bool))
    w = jnp.where(causal[None], w_s, jnp.zeros_like(w_s))
    s = jnp.einsum('gts,bnsgc->bntgc', w, v) + b_s.T[None, None, :, :, None]
    return u * s.reshape(B, S, TOK_WIDTH)


def causal_mla(q_nope, q_rope, k_nope, k_rope, v):
    B, S, H, _ = q_nope.shape
    nb = S // Q_BLOCK
    qn = q_nope.reshape(B, nb, Q_BLOCK, H, QK_NOPE).transpose(1, 0, 2, 3, 4)
    qr = q_rope.reshape(B, nb, Q_BLOCK, H, QK_ROPE).transpose(1, 0, 2, 3, 4)
    k_idx = jnp.arange(S)

    def block(args):
        i, qn_b, qr_b = args
        s = (jnp.einsum('bqhd,bkhd->bhqk', qn_b, k_nope)
             + jnp.einsum('bqhr,bkr->bhqk', qr_b, k_rope)).astype(jnp.float32) * MLA_SCALE
        q_idx = i * Q_BLOCK + jnp.arange(Q_BLOCK)
        mask = q_idx[:, None] >= k_idx[None, :]
        s = jnp.where(mask[None, None], s, jnp.finfo(jnp.float32).min)
        p = jax.nn.softmax(s, axis=-1).astype(v.dtype)
        return jnp.einsum('bhqk,bkhd->bqhd', p, v)

    out = lax.map(block, (jnp.arange(nb), qn, qr))
    return out.transpose(1, 0, 2, 3, 4).reshape(B, S, H * V_DIM)


def shared_latent_kv(h, kv_src_norm, w_kv_a, kv_norm, w_uk, w_uv, cos, sin):
    B, S, _ = h.shape
    a = rms_norm(h, kv_src_norm) @ w_kv_a
    c_kv = rms_norm(a[..., :KV_LORA], kv_norm)
    k_rope = apply_rope(a[..., KV_LORA:], cos, sin)
    k_nope = (c_kv @ w_uk).reshape(B, S, MLA_HEADS, QK_NOPE)
    v = (c_kv @ w_uv).reshape(B, S, MLA_HEADS, V_DIM)
    return k_nope, k_rope, v


def setup_inputs(seed: int = 0) -> dict:
    key = jax.random.key(seed)
    ks = iter(jax.random.split(key, 32))
    f32 = jnp.float32

    def w(shape, fan_in):
        return jax.random.normal(next(ks), shape, f32) * (fan_in ** -0.5)

    def gain(shape):
        return 1.0 + 0.02 * jax.random.normal(next(ks), shape, f32)

    x = jax.random.normal(next(ks), (BATCH, SEQ, D_MODEL), f32)
    mem = jax.random.normal(next(ks), (BATCH, MEM_LEN, D_MODEL), f32)
    offset = jax.random.randint(next(ks), (BATCH, 1), 0, 4096, dtype=jnp.int32)
    positions = offset + jnp.arange(SEQ, dtype=jnp.int32)[None, :]
    return {
        "x": x,
        "mem": mem,
        "positions": positions,
        "norm_gains": gain((DEPTH, 4, D_MODEL)),
        "mem_norm": gain((DEPTH, D_MODEL)),
        "w_mem_kv": w((DEPTH, D_MODEL, 2 * MEM_WIDTH), D_MODEL),
        "ffn_w_gate": w((DEPTH, D_MODEL, D_FF), D_MODEL),
        "ffn_w_up": w((DEPTH, D_MODEL, D_FF), D_MODEL),
        "ffn_w_down": w((DEPTH, D_FF, D_MODEL), D_FF),
        "a_w_in": w((N_A_LAYERS, D_MODEL, A_IN_WIDTH), D_MODEL),
        "a_ln_g": gain((N_A_LAYERS, TOK_WIDTH)),
        "a_ln_b": 0.02 * jax.random.normal(next(ks), (N_A_LAYERS, TOK_WIDTH), f32),
        "a_w_s": w((N_A_LAYERS, SGU_GROUPS, CHUNK, CHUNK), CHUNK),
        "a_b_s": gain((N_A_LAYERS, SGU_GROUPS, CHUNK)),
        "a_w_out": w((N_A_LAYERS, MIX_WIDTH, D_MODEL), MIX_WIDTH),
        "kv_src_norm": gain((D_MODEL,)),
        "w_kv_a": w((D_MODEL, KV_LORA + QK_ROPE), D_MODEL),
        "kv_norm": gain((KV_LORA,)),
        "w_uk": w((KV_LORA, MLA_HEADS * QK_NOPE), KV_LORA),
        "w_uv": w((KV_LORA, MLA_HEADS * V_DIM), KV_LORA),
        "b_w_in": w((N_B_LAYERS, D_MODEL, B_IN_WIDTH), D_MODEL),
        "b_q_norm": gain((N_B_LAYERS, Q_LORA)),
        "b_w_uq": w((N_B_LAYERS, Q_LORA, MLA_HEADS * (QK_NOPE + QK_ROPE)), Q_LORA),
        "b_w_out": w((N_B_LAYERS, MIX_WIDTH, D_MODEL), MIX_WIDTH),
    }


def reference(x, mem, positions, norm_gains, mem_norm, w_mem_kv, ffn_w_gate, ffn_w_up,
              ffn_w_down, a_w_in, a_ln_g, a_ln_b, a_w_s, a_b_s, a_w_out, kv_src_norm,
              w_kv_a, kv_norm, w_uk, w_uv, b_w_in, b_q_norm, b_w_uq, b_w_out):
    B, S, _ = x.shape
    cos, sin = rope_tables(positions)
    h = x
    k_nope = k_rope = v_shared = None
    for i in range(DEPTH):
        hn = rms_norm(h, norm_gains[i, 0])
        if i < N_A_LAYERS:
            j = i
            z = hn @ a_w_in[j]
            tok = gmlp_spatial_gating(z[..., :2 * TOK_WIDTH], a_ln_g[j], a_ln_b[j],
                                      a_w_s[j], a_b_s[j])
            memo = memory_attention(z[..., 2 * TOK_WIDTH:], mem, mem_norm[i], w_mem_kv[i])
            mix = jnp.concatenate([tok, memo], axis=-1) @ a_w_out[j]
        else:
            j = i - N_A_LAYERS
            if j == 0:
                k_nope, k_rope, v_shared = shared_latent_kv(
                    h, kv_src_norm, w_kv_a, kv_norm, w_uk, w_uv, cos, sin)
            z = hn @ b_w_in[j]
            cq = rms_norm(z[..., :Q_LORA], b_q_norm[j])
            q = (cq @ b_w_uq[j]).reshape(B, S, MLA_HEADS, QK_NOPE + QK_ROPE)
            q_nope = q[..., :QK_NOPE]
            q_rope = apply_rope(q[..., QK_NOPE:], cos[:, :, None, :], sin[:, :, None, :])
            att = causal_mla(q_nope, q_rope, k_nope, k_rope, v_shared)
            memo = memory_attention(z[..., Q_LORA:], mem, mem_norm[i], w_mem_kv[i])
            mix = jnp.concatenate([att, memo], axis=-1) @ b_w_out[j]
        h = h + rms_norm(mix, norm_gains[i, 1])
        f = swiglu_ffn(rms_norm(h, norm_gains[i, 2]), ffn_w_gate[i], ffn_w_up[i], ffn_w_down[i])
        h = h + rms_norm(f, norm_gains[i, 3])
    return h
```

```python
import functools

import jax
import jax.numpy as jnp
from jax import lax
from jax.experimental import pallas as pl
from jax.experimental.pallas import tpu as pltpu

D_MODEL = 2048
HEAD_DIM = 128
MEM_LEN = 256
MEM_HEADS = 4
MEM_WIDTH = MEM_HEADS * HEAD_DIM
TOK_WIDTH = D_MODEL - MEM_WIDTH
CHUNK = 128
SGU_GROUPS = TOK_WIDTH // HEAD_DIM
MLA_HEADS = TOK_WIDTH // HEAD_DIM
QK_NOPE = 128
QK_ROPE = 64
V_DIM = 128
Q_LORA = 512
KV_LORA = 512
ROPE_THETA = 10000.0
EPS = 1e-6
MLA_SCALE = (QK_NOPE + QK_ROPE) ** -0.5
MEM_SCALE = HEAD_DIM ** -0.5

LANES = 128
QK_PAD = 2 * LANES
KVA_PAD = KV_LORA + LANES
VMEM_LIMIT = 56 * 1024 * 1024

BF16 = jnp.bfloat16
F32 = jnp.float32


def _dot(a, b):
    return jnp.dot(a, b, preferred_element_type=F32)


def _dot_nt(a, b):
    return lax.dot_general(a, b, (((1,), (1,)), ((), ())), preferred_element_type=F32)


def _rms(x, g):
    return x * lax.rsqrt(jnp.mean(x * x, axis=-1, keepdims=True) + EPS) * g


def _gelu_tanh(x):
    c = 0.7978845608028654
    return x * (0.5 * (1.0 + jnp.tanh(c * (x + 0.044715 * (x * x * x)))))


def _resident(shape):
    nd = len(shape)
    return pl.BlockSpec(shape, lambda *_: (0,) * nd, pipeline_mode=pl.Buffered(1))


def _params(sem):
    return pltpu.CompilerParams(dimension_semantics=sem, vmem_limit_bytes=VMEM_LIMIT)


def _mem_kv_kernel(mem_ref, g_ref, w_ref, o_ref):
    mn = _rms(mem_ref[...], g_ref[0]).astype(BF16)
    o_ref[0] = _dot(mn, w_ref[0]).astype(BF16)


def _mem_kv(mem2d, mem_norm, w_mem_kv_bf):
    depth = w_mem_kv_bf.shape[0]
    rows = mem2d.shape[0]
    return pl.pallas_call(
        _mem_kv_kernel,
        grid=(depth,),
        in_specs=[
            pl.BlockSpec((rows, D_MODEL), lambda l: (0, 0)),
            pl.BlockSpec((1, 1, D_MODEL), lambda l: (l, 0, 0)),
            pl.BlockSpec((1, D_MODEL, 2 * MEM_WIDTH), lambda l: (l, 0, 0)),
        ],
        out_specs=pl.BlockSpec((1, rows, 2 * MEM_WIDTH), lambda l: (l, 0, 0)),
        out_shape=jax.ShapeDtypeStruct((depth, rows, 2 * MEM_WIDTH), BF16),
        compiler_params=_params(("arbitrary",)),
        name="mem_kv",
    )(mem2d, mem_norm.reshape(depth, 1, D_MODEL), w_mem_kv_bf)


def _a_front_kernel(x_ref, g_ref, w_ref, lng_ref, lnb_ref, ws_ref, bst_ref,
                    tok_ref, qm_ref, u_scr, vn_scr):
    tm = x_ref.shape[0]
    hn = _rms(x_ref[...], g_ref[...]).astype(BF16)
    u_scr[...] = _gelu_tanh(_dot(hn, w_ref[:, 0:TOK_WIDTH]))
    v = _gelu_tanh(_dot(hn, w_ref[:, TOK_WIDTH:2 * TOK_WIDTH]))
    qm_ref[...] = _dot(hn, w_ref[:, 2 * TOK_WIDTH:]).astype(BF16)
    mu = jnp.mean(v, axis=-1, keepdims=True)
    vc = v - mu
    vn = vc * lax.rsqrt(jnp.mean(vc * vc, axis=-1, keepdims=True) + EPS)
    vn_scr[...] = (vn * lng_ref[...] + lnb_ref[...]).astype(BF16)

    n_chunks = tm // CHUNK
    row = lax.broadcasted_iota(jnp.int32, (CHUNK, CHUNK), 0)
    col = lax.broadcasted_iota(jnp.int32, (CHUNK, CHUNK), 1)
    causal = row >= col
    for g in range(SGU_GROUPS):
        cols = slice(g * HEAD_DIM, (g + 1) * HEAD_DIM)
        wg = jnp.where(causal, ws_ref[g], 0.0).astype(BF16)
        rhs = jnp.concatenate(
            [vn_scr[c * CHUNK:(c + 1) * CHUNK, cols] for c in range(n_chunks)], axis=1)
        s = _dot(wg, rhs) + bst_ref[:, g:g + 1]
        for c in range(n_chunks):
            rows = slice(c * CHUNK, (c + 1) * CHUNK)
            tok_ref[rows, cols] = (u_scr[rows, cols] * s[:, c * CHUNK:(c + 1) * CHUNK]).astype(BF16)


def _a_front(h, gain, w_in_bf, ln_g, ln_b, w_s, b_s_t, tm):
    t = h.shape[0]
    a_in = w_in_bf.shape[1]
    return pl.pallas_call(
        _a_front_kernel,
        grid=(t // tm,),
        in_specs=[
            pl.BlockSpec((tm, D_MODEL), lambda i: (i, 0)),
            _resident((1, D_MODEL)),
            _resident((D_MODEL, a_in)),
            _resident((1, TOK_WIDTH)),
            _resident((1, TOK_WIDTH)),
            _resident((SGU_GROUPS, CHUNK, CHUNK)),
            _resident((CHUNK, SGU_GROUPS)),
        ],
        out_specs=[
            pl.BlockSpec((tm, TOK_WIDTH), lambda i: (i, 0)),
            pl.BlockSpec((tm, MEM_WIDTH), lambda i: (i, 0)),
        ],
        out_shape=[
            jax.ShapeDtypeStruct((t, TOK_WIDTH), BF16),
            jax.ShapeDtypeStruct((t, MEM_WIDTH), BF16),
        ],
        scratch_shapes=[
            pltpu.VMEM((tm, TOK_WIDTH), F32),
            pltpu.VMEM((tm, TOK_WIDTH), BF16),
        ],
        compiler_params=_params(("parallel",)),
        name="a_front",
    )(h, gain.reshape(1, D_MODEL), w_in_bf, ln_g.reshape(1, TOK_WIDTH),
      ln_b.reshape(1, TOK_WIDTH), w_s, b_s_t)


def _tail_kernel(p1_ref, qm_ref, kv_ref, h_ref, w_ref, g_ref, o_ref, cat_scr):
    cat_scr[:, 0:TOK_WIDTH] = p1_ref[...]
    for hh in range(MEM_HEADS):
        cols = slice(hh * HEAD_DIM, (hh + 1) * HEAD_DIM)
        q = qm_ref[:, cols]
        k = kv_ref[0, :, cols]
        v = kv_ref[0, :, MEM_WIDTH + hh * HEAD_DIM:MEM_WIDTH + (hh + 1) * HEAD_DIM]
        s = _dot_nt(q, k) * MEM_SCALE
        p = jnp.exp(s - jnp.max(s, axis=-1, keepdims=True))
        l = jnp.sum(p, axis=-1, keepdims=True)
        o = _dot(p.astype(BF16), v) / l
        cat_scr[:, TOK_WIDTH + hh * HEAD_DIM:TOK_WIDTH + (hh + 1) * HEAD_DIM] = o.astype(BF16)
    mix = _dot(cat_scr[...], w_ref[...])
    o_ref[...] = h_ref[...] + _rms(mix, g_ref[...])


def _tail(p1, qm, kvm, h, w_out_bf, gain, tm, seq):
    t = h.shape[0]
    per_batch = seq // tm
    return pl.pallas_call(
        _tail_kernel,
        grid=(t // tm,),
        in_specs=[
            pl.BlockSpec((tm, TOK_WIDTH), lambda i: (i, 0)),
            pl.BlockSpec((tm, MEM_WIDTH), lambda i: (i, 0)),
            pl.BlockSpec((1, MEM_LEN, 2 * MEM_WIDTH), lambda i: (i // per_batch, 0, 0)),
            pl.BlockSpec((tm, D_MODEL), lambda i: (i, 0)),
            _resident((D_MODEL, D_MODEL)),
            _resident((1, D_MODEL)),
        ],
        out_specs=pl.BlockSpec((tm, D_MODEL), lambda i: (i, 0)),
        out_shape=jax.ShapeDtypeStruct((t, D_MODEL), F32),
        scratch_shapes=[pltpu.VMEM((tm, D_MODEL), BF16)],
        compiler_params=_params(("parallel",)),
        name="mixer_tail",
    )(p1, qm, kvm, h, w_out_bf, gain.reshape(1, D_MODEL))


def _ffn_kernel(h_ref, gin_ref, wg_ref, wu_ref, wd_ref, gout_ref, o_ref, hn_scr):
    k = pl.program_id(1)

    @pl.when(k == 0)
    def _():
        hn_scr[...] = _rms(h_ref[...], gin_ref[...]).astype(BF16)

    hn = hn_scr[...]
    gate = _dot(hn, wg_ref[...])
    up = _dot(hn, wu_ref[...])
    act = (gate * (1.0 / (1.0 + jnp.exp(-gate))) * up).astype(BF16)
    part = _dot(act, wd_ref[...])

    @pl.when(k == 0)
    def _():
        o_ref[...] = part

    @pl.when(k > 0)
    def _():
        o_ref[...] += part

    @pl.when(k == pl.num_programs(1) - 1)
    def _():
        o_ref[...] = h_ref[...] + _rms(o_ref[...], gout_ref[...])


def _ffn(h, g_in, wg_bf, wu_bf, wd_bf, g_out, tm, tk):
    t = h.shape[0]
    d_ff = wg_bf.shape[1]
    return pl.pallas_call(
        _ffn_kernel,
        grid=(t // tm, d_ff // tk),
        in_specs=[
            pl.BlockSpec((tm, D_MODEL), lambda i, k: (i, 0), pipeline_mode=pl.Buffered(1)),
            _resident((1, D_MODEL)),
            pl.BlockSpec((D_MODEL, tk), lambda i, k: (0, k)),
            pl.BlockSpec((D_MODEL, tk), lambda i, k: (0, k)),
            pl.BlockSpec((tk, D_MODEL), lambda i, k: (k, 0)),
            _resident((1, D_MODEL)),
        ],
        out_specs=pl.BlockSpec((tm, D_MODEL), lambda i, k: (i, 0)),
        out_shape=jax.ShapeDtypeStruct((t, D_MODEL), F32),
        scratch_shapes=[pltpu.VMEM((tm, D_MODEL), BF16)],
        compiler_params=_params(("parallel", "arbitrary")),
        name="swiglu_ffn",
    )(h, g_in.reshape(1, D_MODEL), wg_bf, wu_bf, wd_bf, g_out.reshape(1, D_MODEL))


def _rope_group(x, cos, sin_signed, first_half):
    swapped = jnp.where(first_half, pltpu.roll(x, 96, 1), pltpu.roll(x, 32, 1))
    return x * cos + swapped * sin_signed


def _b_prep_kernel(h_ref, pos_ref, invf_ref, gk_ref, gq_ref, gc_ref, gqn_ref,
                   wkva_ref, wuk_ref, wuv_ref, win_ref, wuq_ref,
                   k_ref, v_ref, q_ref, qm_ref):
    tm = h_ref.shape[0]
    x = h_ref[...]
    y = x * lax.rsqrt(jnp.mean(x * x, axis=-1, keepdims=True) + EPS)
    hk = (y * gk_ref[...]).astype(BF16)
    hq = (y * gq_ref[...]).astype(BF16)

    ang = pos_ref[...].astype(F32) * invf_ref[...]
    lane = lax.broadcasted_iota(jnp.int32, (tm, LANES), 1)
    first_half = (lane % QK_ROPE) < (QK_ROPE // 2)
    cos = jnp.cos(ang)
    sin = jnp.sin(ang)
    sin_signed = jnp.where(first_half, -sin, sin)

    a = _dot(hk, wkva_ref[...])
    ckv = _rms(a[:, 0:KV_LORA], gc_ref[...]).astype(BF16)
    k_rope = _rope_group(a[:, KV_LORA:KVA_PAD], cos, sin_signed, first_half).astype(BF16)
    k_nope = _dot(ckv, wuk_ref[...])
    v_ref[...] = _dot(ckv, wuv_ref[...]).astype(BF16)

    z = _dot(hq, win_ref[...])
    qm_ref[...] = z[:, Q_LORA:].astype(BF16)
    cq = _rms(z[:, 0:Q_LORA], gqn_ref[...]).astype(BF16)
    q = _dot(cq, wuq_ref[...])
    for hh in range(MLA_HEADS):
        lo = hh * QK_PAD
        k_ref[:, lo:lo + LANES] = k_nope[:, hh * QK_NOPE:(hh + 1) * QK_NOPE].astype(BF16)
        k_ref[:, lo + LANES:lo + QK_PAD] = k_rope
        q_ref[:, lo:lo + LANES] = (q[:, lo:lo + LANES] * MLA_SCALE).astype(BF16)
        q_rot = _rope_group(q[:, lo + LANES:lo + QK_PAD], cos, sin_signed, first_half)
        q_ref[:, lo + LANES:lo + QK_PAD] = (q_rot * MLA_SCALE).astype(BF16)


def _b_prep(h, pos_col, inv_freq, g_kv, g_q, g_ckv, g_cq, wkva_bf, wuk_bf, wuv_bf,
            win_bf, wuq_bf, tm):
    t = h.shape[0]
    return pl.pallas_call(
        _b_prep_kernel,
        grid=(t // tm,),
        in_specs=[
            pl.BlockSpec((tm, D_MODEL), lambda i: (i, 0)),
            pl.BlockSpec((tm, 1), lambda i: (i, 0)),
            _resident((1, LANES)),
            _resident((1, D_MODEL)),
            _resident((1, D_MODEL)),
            _resident((1, KV_LORA)),
            _resident((1, Q_LORA)),
            _resident(wkva_bf.shape),
            _resident(wuk_bf.shape),
            _resident(wuv_bf.shape),
            _resident(win_bf.shape),
            _resident(wuq_bf.shape),
        ],
        out_specs=[
            pl.BlockSpec((tm, MLA_HEADS * QK_PAD), lambda i: (i, 0)),
            pl.BlockSpec((tm, MLA_HEADS * V_DIM), lambda i: (i, 0)),
            pl.BlockSpec((tm, MLA_HEADS * QK_PAD), lambda i: (i, 0)),
            pl.BlockSpec((tm, MEM_WIDTH), lambda i: (i, 0)),
        ],
        out_shape=[
            jax.ShapeDtypeStruct((t, MLA_HEADS * QK_PAD), BF16),
            jax.ShapeDtypeStruct((t, MLA_HEADS * V_DIM), BF16),
            jax.ShapeDtypeStruct((t, MLA_HEADS * QK_PAD), BF16),
            jax.ShapeDtypeStruct((t, MEM_WIDTH), BF16),
        ],
        compiler_params=_params(("parallel",)),
        name="b_prep",
    )(h, pos_col, inv_freq, g_kv.reshape(1, D_MODEL), g_q.reshape(1, D_MODEL),
      g_ckv.reshape(1, KV_LORA), g_cq.reshape(1, Q_LORA),
      wkva_bf, wuk_bf, wuv_bf, win_bf, wuq_bf)


def _mla_kernel(q_ref, k_ref, v_ref, o_ref, *, tq, tk):
    qi = pl.program_id(2)
    q = q_ref[0]
    neg = jnp.finfo(F32).min

    def step(j, carry, masked):
        m, l, acc = carry
        start = pl.multiple_of(j * tk, tk)
        k = k_ref[0, pl.ds(start, tk), :]
        v = v_ref[0, pl.ds(start, tk), :]
        s = _dot_nt(q, k)
        if masked:
            q_idx = qi * tq + lax.broadcasted_iota(jnp.int32, (tq, tk), 0)
            k_idx = j * tk + lax.broadcasted_iota(jnp.int32, (tq, tk), 1)
            s = jnp.where(q_idx >= k_idx, s, neg)
        m_new = jnp.maximum(m, jnp.max(s, axis=-1, keepdims=True))
        alpha = jnp.exp(m - m_new)
        p = jnp.exp(s - m_new)
        l = alpha * l + jnp.sum(p, axis=-1, keepdims=True)
        acc = alpha * acc + _dot(p.astype(BF16), v)
        return m_new, l, acc

    init = (jnp.full((tq, 1), neg, F32), jnp.zeros((tq, 1), F32), jnp.zeros((tq, V_DIM), F32))
    n_full = qi * (tq // tk)
    carry = lax.fori_loop(0, n_full, lambda j, c: step(j, c, False), init)
    for d in range(tq // tk):
        carry = step(n_full + d, carry, True)
    _, l, acc = carry
    o_ref[0] = (acc / l).astype(BF16)


def _mla_attn(q, k, v, tq, tk):
    b, s, _ = q.shape
    kern = functools.partial(_mla_kernel, tq=tq, tk=tk)
    return pl.pallas_call(
        kern,
        grid=(b, MLA_HEADS, s // tq),
        in_specs=[
            pl.BlockSpec((1, tq, QK_PAD), lambda bi, hi, qi: (bi, qi, hi)),
            pl.BlockSpec((1, s, QK_PAD), lambda bi, hi, qi: (bi, 0, hi)),
            pl.BlockSpec((1, s, V_DIM), lambda bi, hi, qi: (bi, 0, hi)),
        ],
        out_specs=pl.BlockSpec((1, tq, V_DIM), lambda bi, hi, qi: (bi, qi, hi)),
        out_shape=jax.ShapeDtypeStruct((b, s, MLA_HEADS * V_DIM), BF16),
        compiler_params=_params(("parallel", "parallel", "arbitrary")),
        name="mla_attn",
    )(q, k, v)


def kernel(x, mem, positions, norm_gains, mem_norm, w_mem_kv, ffn_w_gate, ffn_w_up, ffn_w_down,
           a_w_in, a_ln_g, a_ln_b, a_w_s, a_b_s, a_w_out, kv_src_norm, w_kv_a, kv_norm, w_uk, w_uv,
           b_w_in, b_q_norm, b_w_uq, b_w_out):
    b, s, d = x.shape
    t = b * s
    depth = norm_gains.shape[0]
    n_a = a_w_in.shape[0]
    assert d == D_MODEL and depth == 2 and n_a == 1 and b_w_in.shape[0] == 1

    bf = lambda w: w.astype(BF16)
    wuq = b_w_uq[0].reshape(Q_LORA, MLA_HEADS, QK_NOPE + QK_ROPE)
    wuq = jnp.pad(wuq, ((0, 0), (0, 0), (0, QK_PAD - QK_NOPE - QK_ROPE)))
    wuq_bf = bf(wuq.reshape(Q_LORA, MLA_HEADS * QK_PAD))
    wkva_bf = bf(jnp.pad(w_kv_a, ((0, 0), (0, KVA_PAD - KV_LORA - QK_ROPE))))
    inv_freq = ROPE_THETA ** (-jnp.arange(0, QK_ROPE, 2, dtype=F32) / QK_ROPE)
    inv_freq = jnp.tile(inv_freq, LANES // (QK_ROPE // 2)).reshape(1, LANES)

    h = x.reshape(t, d)
    kvm = _mem_kv(mem.reshape(b * MEM_LEN, d), mem_norm, bf(w_mem_kv))
    kvm = kvm.reshape(depth, b, MEM_LEN, 2 * MEM_WIDTH)

    tok, qm = _a_front(h, norm_gains[0, 0], bf(a_w_in[0]), a_ln_g[0], a_ln_b[0], a_w_s[0],
                       a_b_s[0].T, tm=512)
    h = _tail(tok, qm, kvm[0], h, bf(a_w_out[0]), norm_gains[0, 1], tm=512, seq=s)
    h = _ffn(h, norm_gains[0, 2], bf(ffn_w_gate[0]), bf(ffn_w_up[0]), bf(ffn_w_down[0]),
             norm_gains[0, 3], tm=1024, tk=512)

    k_cat, v_all, q_cat, qm = _b_prep(
        h, positions.reshape(t, 1), inv_freq, kv_src_norm, norm_gains[1, 0], kv_norm, b_q_norm[0],
        wkva_bf, bf(w_uk), bf(w_uv), bf(b_w_in[0]), wuq_bf, tm=256)
    att = _mla_attn(q_cat.reshape(b, s, -1), k_cat.reshape(b, s, -1), v_all.reshape(b, s, -1),
                    tq=512, tk=512)
    h = _tail(att.reshape(t, TOK_WIDTH), qm, kvm[1], h, bf(b_w_out[0]), norm_gains[1, 1],
              tm=512, seq=s)
    h = _ffn(h, norm_gains[1, 2], bf(ffn_w_gate[1]), bf(ffn_w_up[1]), bf(ffn_w_down[1]),
             norm_gains[1, 3], tm=1024, tk=512)
    return h.reshape(b, s, d)
```

```python
import functools

import jax
import jax.numpy as jnp
from jax import lax
from jax.experimental import pallas as pl
from jax.experimental.pallas import tpu as pltpu

D_MODEL = 2048
HEAD_DIM = 128
MEM_LEN = 256
MEM_HEADS = 4
MEM_WIDTH = MEM_HEADS * HEAD_DIM
TOK_WIDTH = D_MODEL - MEM_WIDTH
CHUNK = 128
SGU_GROUPS = TOK_WIDTH // HEAD_DIM
MLA_HEADS = TOK_WIDTH // HEAD_DIM
QK_NOPE = 128
QK_ROPE = 64
V_DIM = 128
Q_LORA = 512
KV_LORA = 512
ROPE_THETA = 10000.0
EPS = 1e-6
MLA_SCALE = (QK_NOPE + QK_ROPE) ** -0.5
MEM_SCALE = HEAD_DIM ** -0.5
LOG2_E = 1.4426950408889634
Q_SCALE = MLA_SCALE * LOG2_E

LANES = 128
QK_PAD = 2 * LANES
KVA_PAD = KV_LORA + LANES
VMEM_LIMIT = 56 * 1024 * 1024

BF16 = jnp.bfloat16
F32 = jnp.float32


def _dot(a, b):
    return jnp.dot(a, b, preferred_element_type=F32)


def _dot_nt(a, b):
    return lax.dot_general(a, b, (((1,), (1,)), ((), ())), preferred_element_type=F32)


def _rms(x, g):
    return x * lax.rsqrt(jnp.mean(x * x, axis=-1, keepdims=True) + EPS) * g


def _gelu_tanh(x):
    c = 0.7978845608028654
    return x * (0.5 * (1.0 + jnp.tanh(c * (x + 0.044715 * (x * x * x)))))


def _resident(shape):
    nd = len(shape)
    return pl.BlockSpec(shape, lambda *_: (0,) * nd, pipeline_mode=pl.Buffered(1))


def _params(sem):
    return pltpu.CompilerParams(dimension_semantics=sem, vmem_limit_bytes=VMEM_LIMIT)


def _mem_kv_kernel(mem_ref, g_ref, w_ref, o_ref):
    mn = _rms(mem_ref[...], g_ref[0]).astype(BF16)
    o_ref[0] = _dot(mn, w_ref[0]).astype(BF16)


def _mem_kv(mem2d, mem_norm, w_mem_kv_bf):
    depth = w_mem_kv_bf.shape[0]
    rows = mem2d.shape[0]
    return pl.pallas_call(
        _mem_kv_kernel,
        grid=(depth,),
        in_specs=[
            pl.BlockSpec((rows, D_MODEL), lambda l: (0, 0)),
            pl.BlockSpec((1, 1, D_MODEL), lambda l: (l, 0, 0)),
            pl.BlockSpec((1, D_MODEL, 2 * MEM_WIDTH), lambda l: (l, 0, 0)),
        ],
        out_specs=pl.BlockSpec((1, rows, 2 * MEM_WIDTH), lambda l: (l, 0, 0)),
        out_shape=jax.ShapeDtypeStruct((depth, rows, 2 * MEM_WIDTH), BF16),
        compiler_params=_params(("arbitrary",)),
        name="mem_kv",
    )(mem2d, mem_norm.reshape(depth, 1, D_MODEL), w_mem_kv_bf)


def _a_front_kernel(x_ref, g_ref, w_ref, lng_ref, lnb_ref, ws_ref, bst_ref,
                    tok_ref, qm_ref, u_scr, vn_scr):
    tm = x_ref.shape[0]
    hn = _rms(x_ref[...], g_ref[...]).astype(BF16)
    u_scr[...] = _gelu_tanh(_dot(hn, w_ref[:, 0:TOK_WIDTH]))
    v = _gelu_tanh(_dot(hn, w_ref[:, TOK_WIDTH:2 * TOK_WIDTH]))
    qm_ref[...] = _dot(hn, w_ref[:, 2 * TOK_WIDTH:]).astype(BF16)
    mu = jnp.mean(v, axis=-1, keepdims=True)
    vc = v - mu
    vn = vc * lax.rsqrt(jnp.mean(vc * vc, axis=-1, keepdims=True) + EPS)
    vn_scr[...] = (vn * lng_ref[...] + lnb_ref[...]).astype(BF16)

    n_chunks = tm // CHUNK
    row = lax.broadcasted_iota(jnp.int32, (CHUNK, CHUNK), 0)
    col = lax.broadcasted_iota(jnp.int32, (CHUNK, CHUNK), 1)
    causal = row >= col
    for g in range(SGU_GROUPS):
        cols = slice(g * HEAD_DIM, (g + 1) * HEAD_DIM)
        wg = jnp.where(causal, ws_ref[g], 0.0).astype(BF16)
        rhs = jnp.concatenate(
            [vn_scr[c * CHUNK:(c + 1) * CHUNK, cols] for c in range(n_chunks)], axis=1)
        s = _dot(wg, rhs) + bst_ref[:, g:g + 1]
        for c in range(n_chunks):
            rows = slice(c * CHUNK, (c + 1) * CHUNK)
            tok_ref[rows, cols] = (u_scr[rows, cols] * s[:, c * CHUNK:(c + 1) * CHUNK]).astype(BF16)


def _a_front(h, gain, w_in_bf, ln_g, ln_b, w_s, b_s_t, tm):
    t = h.shape[0]
    a_in = w_in_bf.shape[1]
    return pl.pallas_call(
        _a_front_kernel,
        grid=(t // tm,),
        in_specs=[
            pl.BlockSpec((tm, D_MODEL), lambda i: (i, 0)),
            _resident((1, D_MODEL)),
            _resident((D_MODEL, a_in)),
            _resident((1, TOK_WIDTH)),
            _resident((1, TOK_WIDTH)),
            _resident((SGU_GROUPS, CHUNK, CHUNK)),
            _resident((CHUNK, SGU_GROUPS)),
        ],
        out_specs=[
            pl.BlockSpec((tm, TOK_WIDTH), lambda i: (i, 0)),
            pl.BlockSpec((tm, MEM_WIDTH), lambda i: (i, 0)),
        ],
        out_shape=[
            jax.ShapeDtypeStruct((t, TOK_WIDTH), BF16),
            jax.ShapeDtypeStruct((t, MEM_WIDTH), BF16),
        ],
        scratch_shapes=[
            pltpu.VMEM((tm, TOK_WIDTH), F32),
            pltpu.VMEM((tm, TOK_WIDTH), BF16),
        ],
        compiler_params=_params(("parallel",)),
        name="a_front",
    )(h, gain.reshape(1, D_MODEL), w_in_bf, ln_g.reshape(1, TOK_WIDTH),
      ln_b.reshape(1, TOK_WIDTH), w_s, b_s_t)


def _tail_kernel(p1_ref, qm_ref, kv_ref, h_ref, w_ref, g_ref, o_ref, cat_scr):
    cat_scr[:, 0:TOK_WIDTH] = p1_ref[...]
    for hh in range(MEM_HEADS):
        cols = slice(hh * HEAD_DIM, (hh + 1) * HEAD_DIM)
        q = qm_ref[:, cols]
        k = kv_ref[0, :, cols]
        v = kv_ref[0, :, MEM_WIDTH + hh * HEAD_DIM:MEM_WIDTH + (hh + 1) * HEAD_DIM]
        s = _dot_nt(q, k) * MEM_SCALE
        p = jnp.exp(s - jnp.max(s, axis=-1, keepdims=True))
        l = jnp.sum(p, axis=-1, keepdims=True)
        o = _dot(p.astype(BF16), v) / l
        cat_scr[:, TOK_WIDTH + hh * HEAD_DIM:TOK_WIDTH + (hh + 1) * HEAD_DIM] = o.astype(BF16)
    mix = _dot(cat_scr[...], w_ref[...])
    o_ref[...] = h_ref[...] + _rms(mix, g_ref[...])


def _tail(p1, qm, kvm, h, w_out_bf, gain, layer, tm, seq):
    t = h.shape[0]
    per_batch = seq // tm
    return pl.pallas_call(
        _tail_kernel,
        grid=(t // tm,),
        in_specs=[
            pl.BlockSpec((tm, TOK_WIDTH), lambda i: (i, 0)),
            pl.BlockSpec((tm, MEM_WIDTH), lambda i: (i, 0)),
            pl.BlockSpec((None, 1, MEM_LEN, 2 * MEM_WIDTH),
                         lambda i: (layer, i // per_batch, 0, 0)),
            pl.BlockSpec((tm, D_MODEL), lambda i: (i, 0)),
            _resident((D_MODEL, D_MODEL)),
            _resident((1, D_MODEL)),
        ],
        out_specs=pl.BlockSpec((tm, D_MODEL), lambda i: (i, 0)),
        out_shape=jax.ShapeDtypeStruct((t, D_MODEL), F32),
        scratch_shapes=[pltpu.VMEM((tm, D_MODEL), BF16)],
        compiler_params=_params(("parallel",)),
        name="mixer_tail",
    )(p1, qm, kvm, h, w_out_bf, gain.reshape(1, D_MODEL))


def _ffn_kernel(h_ref, gin_ref, wg_ref, wu_ref, wd_ref, gout_ref, o_ref, hn_scr):
    k = pl.program_id(1)

    @pl.when(k == 0)
    def _():
        hn_scr[...] = _rms(h_ref[...], gin_ref[...]).astype(BF16)
        o_ref[...] = jnp.zeros_like(o_ref)

    hn = hn_scr[...]
    gate = _dot(hn, wg_ref[...])
    up = _dot(hn, wu_ref[...])
    act = (gate * (1.0 / (1.0 + jnp.exp(-gate))) * up).astype(BF16)
    o_ref[...] += _dot(act, wd_ref[...])

    @pl.when(k == pl.num_programs(1) - 1)
    def _():
        o_ref[...] = h_ref[...] + _rms(o_ref[...], gout_ref[...])


def _ffn(h, g_in, wg_bf, wu_bf, wd_bf, g_out, layer, tm, tk):
    t = h.shape[0]
    d_ff = wg_bf.shape[2]
    return pl.pallas_call(
        _ffn_kernel,
        grid=(t // tm, d_ff // tk),
        in_specs=[
            pl.BlockSpec((tm, D_MODEL), lambda i, k: (i, 0), pipeline_mode=pl.Buffered(1)),
            _resident((1, D_MODEL)),
            pl.BlockSpec((None, D_MODEL, tk), lambda i, k: (layer, 0, k)),
            pl.BlockSpec((None, D_MODEL, tk), lambda i, k: (layer, 0, k)),
            pl.BlockSpec((None, tk, D_MODEL), lambda i, k: (layer, k, 0)),
            _resident((1, D_MODEL)),
        ],
        out_specs=pl.BlockSpec((tm, D_MODEL), lambda i, k: (i, 0)),
        out_shape=jax.ShapeDtypeStruct((t, D_MODEL), F32),
        scratch_shapes=[pltpu.VMEM((tm, D_MODEL), BF16)],
        compiler_params=_params(("parallel", "arbitrary")),
        name="swiglu_ffn",
    )(h, g_in.reshape(1, D_MODEL), wg_bf, wu_bf, wd_bf, g_out.reshape(1, D_MODEL))


def _rope_group(x, cos, sin_signed, first_half):
    swapped = jnp.where(first_half, pltpu.roll(x, 96, 1), pltpu.roll(x, 32, 1))
    return x * cos + swapped * sin_signed


def _b_prep_kernel(h_ref, pos_ref, invf_ref, gk_ref, gq_ref, gc_ref, gqn_ref,
                   wkva_ref, wuk_ref, wuv_ref, win_ref, wuq_ref,
                   k_ref, v_ref, q_ref, qm_ref):
    tm = h_ref.shape[0]
    x = h_ref[...]
    y = x * lax.rsqrt(jnp.mean(x * x, axis=-1, keepdims=True) + EPS)
    hk = (y * gk_ref[...]).astype(BF16)
    hq = (y * gq_ref[...]).astype(BF16)

    ang = pos_ref[...].astype(F32) * invf_ref[...]
    lane = lax.broadcasted_iota(jnp.int32, (tm, LANES), 1)
    first_half = (lane % QK_ROPE) < (QK_ROPE // 2)
    cos = jnp.cos(ang)
    sin = jnp.sin(ang)
    sin_signed = jnp.where(first_half, -sin, sin)

    a = _dot(hk, wkva_ref[...])
    ckv = _rms(a[:, 0:KV_LORA], gc_ref[...]).astype(BF16)
    k_rope = _rope_group(a[:, KV_LORA:KVA_PAD], cos, sin_signed, first_half).astype(BF16)
    k_nope = _dot(ckv, wuk_ref[...])
    v_ref[...] = _dot(ckv, wuv_ref[...]).astype(BF16)

    z = _dot(hq, win_ref[...])
    qm_ref[...] = z[:, Q_LORA:].astype(BF16)
    cq = _rms(z[:, 0:Q_LORA], gqn_ref[...]).astype(BF16)
    q = _dot(cq, wuq_ref[...])
    for hh in range(MLA_HEADS):
        lo = hh * QK_PAD
        k_ref[:, lo:lo + LANES] = k_nope[:, hh * QK_NOPE:(hh + 1) * QK_NOPE].astype(BF16)
        k_ref[:, lo + LANES:lo + QK_PAD] = k_rope
        q_ref[:, lo:lo + LANES] = (q[:, lo:lo + LANES] * Q_SCALE).astype(BF16)
        q_rot = _rope_group(q[:, lo + LANES:lo + QK_PAD], cos, sin_signed, first_half)
        q_ref[:, lo + LANES:lo + QK_PAD] = (q_rot * Q_SCALE).astype(BF16)


def _b_prep(h, pos_col, inv_freq, g_kv, g_q, g_ckv, g_cq, wkva_bf, wuk_bf, wuv_bf,
            win_bf, wuq_bf, tm):
    t = h.shape[0]
    return pl.pallas_call(
        _b_prep_kernel,
        grid=(t // tm,),
        in_specs=[
            pl.BlockSpec((tm, D_MODEL), lambda i: (i, 0)),
            pl.BlockSpec((tm, 1), lambda i: (i, 0)),
            _resident((1, LANES)),
            _resident((1, D_MODEL)),
            _resident((1, D_MODEL)),
            _resident((1, KV_LORA)),
            _resident((1, Q_LORA)),
            _resident(wkva_bf.shape),
            _resident(wuk_bf.shape),
            _resident(wuv_bf.shape),
            _resident(win_bf.shape),
            _resident(wuq_bf.shape),
        ],
        out_specs=[
            pl.BlockSpec((tm, MLA_HEADS * QK_PAD), lambda i: (i, 0)),
            pl.BlockSpec((tm, MLA_HEADS * V_DIM), lambda i: (i, 0)),
            pl.BlockSpec((tm, MLA_HEADS * QK_PAD), lambda i: (i, 0)),
            pl.BlockSpec((tm, MEM_WIDTH), lambda i: (i, 0)),
        ],
        out_shape=[
            jax.ShapeDtypeStruct((t, MLA_HEADS * QK_PAD), BF16),
            jax.ShapeDtypeStruct((t, MLA_HEADS * V_DIM), BF16),
            jax.ShapeDtypeStruct((t, MLA_HEADS * QK_PAD), BF16),
            jax.ShapeDtypeStruct((t, MEM_WIDTH), BF16),
        ],
        compiler_params=_params(("parallel",)),
        name="b_prep",
    )(h, pos_col, inv_freq, g_kv.reshape(1, D_MODEL), g_q.reshape(1, D_MODEL),
      g_ckv.reshape(1, KV_LORA), g_cq.reshape(1, Q_LORA),
      wkva_bf, wuk_bf, wuv_bf, win_bf, wuq_bf)


def _mla_kernel(q_ref, k_ref, v_ref, o_ref, vaug_scr, *, blk):
    s_len = q_ref.shape[1]
    n_blk = s_len // blk
    neg = jnp.finfo(F32).min
    vaug_scr[:, 0:V_DIM] = v_ref[0]
    lane = lax.broadcasted_iota(jnp.int32, (s_len, LANES), 1)
    vaug_scr[:, V_DIM:] = jnp.where(lane == 0, 1.0, 0.0).astype(BF16)
    row = lax.broadcasted_iota(jnp.int32, (blk, blk), 0)
    col = lax.broadcasted_iota(jnp.int32, (blk, blk), 1)
    causal = row >= col

    def scores(qi, j):
        return _dot_nt(q_ref[0, qi * blk:(qi + 1) * blk, :], k_ref[0, j * blk:(j + 1) * blk, :])

    def accumulate(pending, acc):
        qi, j, p, alpha = pending
        pv = _dot(p, vaug_scr[j * blk:(j + 1) * blk, :])
        acc = pv if j == 0 else alpha * acc + pv
        if j == qi:
            out = acc[:, 0:V_DIM] / acc[:, V_DIM:V_DIM + 1]
            o_ref[0, qi * blk:(qi + 1) * blk, :] = out.astype(BF16)
        return acc

    pairs = [(qi, j) for qi in range(n_blk) for j in range(qi + 1)]
    s_next = scores(*pairs[0])
    m = acc = pending = None
    for t, (qi, j) in enumerate(pairs):
        s = s_next
        if t + 1 < len(pairs):
            s_next = scores(*pairs[t + 1])
        if j == qi:
            s = jnp.where(causal, s, neg)
        s_max = jnp.max(s, axis=-1, keepdims=True)
        if j == 0:
            m_new, alpha = s_max, None
        else:
            m_new = jnp.maximum(m, s_max)
            alpha = jnp.exp2(m - m_new)
        p = jnp.exp2(s - m_new).astype(BF16)
        if pending is not None:
            acc = accumulate(pending, acc)
        pending, m = (qi, j, p, alpha), m_new
    accumulate(pending, acc)


def _mla_attn(q, k, v, blk):
    b, s, _ = q.shape
    kern = functools.partial(_mla_kernel, blk=blk)
    return pl.pallas_call(
        kern,
        grid=(b, MLA_HEADS),
        in_specs=[
            pl.BlockSpec((1, s, QK_PAD), lambda bi, hi: (bi, 0, hi)),
            pl.BlockSpec((1, s, QK_PAD), lambda bi, hi: (bi, 0, hi)),
            pl.BlockSpec((1, s, V_DIM), lambda bi, hi: (bi, 0, hi)),
        ],
        out_specs=pl.BlockSpec((1, s, V_DIM), lambda bi, hi: (bi, 0, hi)),
        out_shape=jax.ShapeDtypeStruct((b, s, MLA_HEADS * V_DIM), BF16),
        scratch_shapes=[pltpu.VMEM((s, 2 * V_DIM), BF16)],
        compiler_params=_params(("parallel", "parallel")),
        name="mla_attn",
    )(q, k, v)


def kernel(x, mem, positions, norm_gains, mem_norm, w_mem_kv, ffn_w_gate, ffn_w_up, ffn_w_down,
           a_w_in, a_ln_g, a_ln_b, a_w_s, a_b_s, a_w_out, kv_src_norm, w_kv_a, kv_norm, w_uk, w_uv,
           b_w_in, b_q_norm, b_w_uq, b_w_out):
    b, s, d = x.shape
    t = b * s
    depth = norm_gains.shape[0]
    n_a = a_w_in.shape[0]
    assert d == D_MODEL and depth == 2 and n_a == 1 and b_w_in.shape[0] == 1

    bf = lambda w: w.astype(BF16)
    wuq = b_w_uq[0].reshape(Q_LORA, MLA_HEADS, QK_NOPE + QK_ROPE)
    wuq = jnp.pad(wuq, ((0, 0), (0, 0), (0, QK_PAD - QK_NOPE - QK_ROPE)))
    wuq_bf = bf(wuq.reshape(Q_LORA, MLA_HEADS * QK_PAD))
    wkva_bf = bf(jnp.pad(w_kv_a, ((0, 0), (0, KVA_PAD - KV_LORA - QK_ROPE))))
    inv_freq = ROPE_THETA ** (-jnp.arange(0, QK_ROPE, 2, dtype=F32) / QK_ROPE)
    inv_freq = jnp.tile(inv_freq, LANES // (QK_ROPE // 2)).reshape(1, LANES)

    h = x.reshape(t, d)
    kvm = _mem_kv(mem.reshape(b * MEM_LEN, d), mem_norm, bf(w_mem_kv))
    kvm = kvm.reshape(depth, b, MEM_LEN, 2 * MEM_WIDTH)

    tok, qm = _a_front(h, norm_gains[0, 0], bf(a_w_in[0]), a_ln_g[0], a_ln_b[0], a_w_s[0],
                       a_b_s[0].T, tm=512)
    h = _tail(tok, qm, kvm, h, bf(a_w_out[0]), norm_gains[0, 1], layer=0, tm=512, seq=s)
    wg_bf, wu_bf, wd_bf = bf(ffn_w_gate), bf(ffn_w_up), bf(ffn_w_down)
    h = _ffn(h, norm_gains[0, 2], wg_bf, wu_bf, wd_bf, norm_gains[0, 3], layer=0, tm=1024, tk=512)

    k_cat, v_all, q_cat, qm = _b_prep(
        h, positions.reshape(t, 1), inv_freq, kv_src_norm, norm_gains[1, 0], kv_norm, b_q_norm[0],
        wkva_bf, bf(w_uk), bf(w_uv), bf(b_w_in[0]), wuq_bf, tm=256)
    att = _mla_attn(q_cat.reshape(b, s, -1), k_cat.reshape(b, s, -1), v_all.reshape(b, s, -1),
                    blk=512)
    h = _tail(att.reshape(t, TOK_WIDTH), qm, kvm, h, bf(b_w_out[0]), norm_gains[1, 1],
              layer=1, tm=512, seq=s)
    h = _ffn(h, norm_gains[1, 2], wg_bf, wu_bf, wd_bf, norm_gains[1, 3], layer=1, tm=1024, tk=512)
    return h.reshape(b, s, d)
```

```python
import functools

import jax
import jax.numpy as jnp
from jax import lax
from jax.experimental import pallas as pl
from jax.experimental.pallas import tpu as pltpu

D_MODEL = 2048
HEAD_DIM = 128
MEM_LEN = 256
MEM_HEADS = 4
MEM_WIDTH = MEM_HEADS * HEAD_DIM
TOK_WIDTH = D_MODEL - MEM_WIDTH
CHUNK = 128
SGU_GROUPS = TOK_WIDTH // HEAD_DIM
MLA_HEADS = TOK_WIDTH // HEAD_DIM
QK_NOPE = 128
QK_ROPE = 64
V_DIM = 128
Q_LORA = 512
KV_LORA = 512
ROPE_THETA = 10000.0
EPS = 1e-6
MLA_SCALE = (QK_NOPE + QK_ROPE) ** -0.5
MEM_SCALE = HEAD_DIM ** -0.5
LOG2_E = 1.4426950408889634
Q_SCALE = MLA_SCALE * LOG2_E

LANES = 128
QK_PAD = 2 * LANES
KVA_PAD = KV_LORA + LANES
VMEM_LIMIT = 56 * 1024 * 1024
ROW_CHUNK = 128

BF16 = jnp.bfloat16
F32 = jnp.float32


def _dot(a, b):
    return jnp.dot(a, b, preferred_element_type=F32)


def _dot_nt(a, b):
    return lax.dot_general(a, b, (((1,), (1,)), ((), ())), preferred_element_type=F32)


def _rms(x, g):
    return x * lax.rsqrt(jnp.mean(x * x, axis=-1, keepdims=True) + EPS) * g


def _gelu_tanh(x):
    c = 0.7978845608028654
    return x * (0.5 * (1.0 + jnp.tanh(c * (x + 0.044715 * (x * x * x)))))


def _resident(shape):
    nd = len(shape)
    return pl.BlockSpec(shape, lambda *_: (0,) * nd, pipeline_mode=pl.Buffered(1))


def _params(sem):
    return pltpu.CompilerParams(dimension_semantics=sem, vmem_limit_bytes=VMEM_LIMIT)


def _mem_kv_kernel(mem_ref, g_ref, w_ref, o_ref):
    mn = _rms(mem_ref[...], g_ref[0]).astype(BF16)
    o_ref[0] = _dot(mn, w_ref[0]).astype(BF16)


def _mem_kv(mem2d, mem_norm, w_mem_kv_bf):
    depth = w_mem_kv_bf.shape[0]
    rows = mem2d.shape[0]
    return pl.pallas_call(
        _mem_kv_kernel,
        grid=(depth,),
        in_specs=[
            pl.BlockSpec((rows, D_MODEL), lambda l: (0, 0)),
            pl.BlockSpec((1, 1, D_MODEL), lambda l: (l, 0, 0)),
            pl.BlockSpec((1, D_MODEL, 2 * MEM_WIDTH), lambda l: (l, 0, 0)),
        ],
        out_specs=pl.BlockSpec((1, rows, 2 * MEM_WIDTH), lambda l: (l, 0, 0)),
        out_shape=jax.ShapeDtypeStruct((depth, rows, 2 * MEM_WIDTH), BF16),
        compiler_params=_params(("arbitrary",)),
        name="mem_kv",
    )(mem2d, mem_norm.reshape(depth, 1, D_MODEL), w_mem_kv_bf)


def _cast_specs(stacked, layer, steps):
    in_specs, out_specs, out_shapes = [], [], []
    for w in stacked:
        _, rows, cols = w.shape
        slab = rows // steps
        assert slab * steps == rows and slab % 16 == 0
        in_specs.append(pl.BlockSpec((None, slab, cols), lambda i: (layer, i, 0)))
        out_specs.append(pl.BlockSpec((slab, cols), lambda i: (i, 0)))
        out_shapes.append(jax.ShapeDtypeStruct((rows, cols), BF16))
    return in_specs, out_specs, out_shapes


def _cast_slabs(src_refs, dst_refs):
    for src, dst in zip(src_refs, dst_refs):
        dst[...] = src[...].astype(BF16)


def _a_front_kernel(x_ref, g_ref, w_ref, lng_ref, lnb_ref, ws_ref, bst_ref, *rest, n_cast):
    cast_in, (tok_ref, qm_ref), rest = rest[:n_cast], rest[n_cast:n_cast + 2], rest[n_cast + 2:]
    cast_out, (u_scr, vn_scr) = rest[:n_cast], rest[n_cast:]
    _cast_slabs(cast_in, cast_out)
    tm = x_ref.shape[0]
    hn = _rms(x_ref[...], g_ref[...]).astype(BF16)
    u_scr[...] = _gelu_tanh(_dot(hn, w_ref[:, 0:TOK_WIDTH]))
    v = _gelu_tanh(_dot(hn, w_ref[:, TOK_WIDTH:2 * TOK_WIDTH]))
    qm_ref[...] = _dot(hn, w_ref[:, 2 * TOK_WIDTH:]).astype(BF16)
    mu = jnp.mean(v, axis=-1, keepdims=True)
    vc = v - mu
    vn = vc * lax.rsqrt(jnp.mean(vc * vc, axis=-1, keepdims=True) + EPS)
    vn_scr[...] = (vn * lng_ref[...] + lnb_ref[...]).astype(BF16)

    n_chunks = tm // CHUNK
    row = lax.broadcasted_iota(jnp.int32, (CHUNK, CHUNK), 0)
    col = lax.broadcasted_iota(jnp.int32, (CHUNK, CHUNK), 1)
    causal = row >= col
    for g in range(SGU_GROUPS):
        cols = slice(g * HEAD_DIM, (g + 1) * HEAD_DIM)
        wg = jnp.where(causal, ws_ref[g], 0.0).astype(BF16)
        rhs = jnp.concatenate(
            [vn_scr[c * CHUNK:(c + 1) * CHUNK, cols] for c in range(n_chunks)], axis=1)
        s = _dot(wg, rhs) + bst_ref[:, g:g + 1]
        for c in range(n_chunks):
            rows = slice(c * CHUNK, (c + 1) * CHUNK)
            tok_ref[rows, cols] = (u_scr[rows, cols] * s[:, c * CHUNK:(c + 1) * CHUNK]).astype(BF16)


def _a_front(h, gain, w_in_bf, ln_g, ln_b, w_s, b_s_t, cast, cast_layer, tm):
    t = h.shape[0]
    a_in = w_in_bf.shape[1]
    c_in, c_out, c_shapes = _cast_specs(cast, cast_layer, t // tm)
    return pl.pallas_call(
        functools.partial(_a_front_kernel, n_cast=len(cast)),
        grid=(t // tm,),
        in_specs=[
            pl.BlockSpec((tm, D_MODEL), lambda i: (i, 0)),
            _resident((1, D_MODEL)),
            _resident((D_MODEL, a_in)),
            _resident((1, TOK_WIDTH)),
            _resident((1, TOK_WIDTH)),
            _resident((SGU_GROUPS, CHUNK, CHUNK)),
            _resident((CHUNK, SGU_GROUPS)),
        ] + c_in,
        out_specs=[
            pl.BlockSpec((tm, TOK_WIDTH), lambda i: (i, 0)),
            pl.BlockSpec((tm, MEM_WIDTH), lambda i: (i, 0)),
        ] + c_out,
        out_shape=[
            jax.ShapeDtypeStruct((t, TOK_WIDTH), BF16),
            jax.ShapeDtypeStruct((t, MEM_WIDTH), BF16),
        ] + c_shapes,
        scratch_shapes=[
            pltpu.VMEM((tm, TOK_WIDTH), F32),
            pltpu.VMEM((tm, TOK_WIDTH), BF16),
        ],
        compiler_params=_params(("parallel",)),
        name="a_front",
    )(h, gain.reshape(1, D_MODEL), w_in_bf, ln_g.reshape(1, TOK_WIDTH),
      ln_b.reshape(1, TOK_WIDTH), w_s, b_s_t, *cast)


def _tail_kernel(p1_ref, qm_ref, kv_ref, h_ref, w_ref, g_ref, *rest, n_cast):
    cast_in, o_ref, cast_out = rest[:n_cast], rest[n_cast], rest[n_cast + 1:]
    _cast_slabs(cast_in, cast_out)
    mix = _dot(p1_ref[...], w_ref[0:TOK_WIDTH, :])
    heads = []
    for hh in range(MEM_HEADS):
        cols = slice(hh * HEAD_DIM, (hh + 1) * HEAD_DIM)
        q = qm_ref[:, cols]
        k = kv_ref[0, :, cols]
        v = kv_ref[0, :, MEM_WIDTH + hh * HEAD_DIM:MEM_WIDTH + (hh + 1) * HEAD_DIM]
        s = _dot_nt(q, k) * MEM_SCALE
        p = jnp.exp(s - jnp.max(s, axis=-1, keepdims=True))
        l = jnp.sum(p, axis=-1, keepdims=True)
        heads.append((_dot(p.astype(BF16), v) / l).astype(BF16))
    mix = mix + _dot(jnp.concatenate(heads, axis=1), w_ref[TOK_WIDTH:, :])
    o_ref[...] = h_ref[...] + _rms(mix, g_ref[...])


def _tail(p1, qm, kvm, h, w_out_bf, gain, cast, cast_layer, layer, tm, seq):
    t = h.shape[0]
    per_batch = seq // tm
    c_in, c_out, c_shapes = _cast_specs(cast, cast_layer, t // tm)
    return pl.pallas_call(
        functools.partial(_tail_kernel, n_cast=len(cast)),
        grid=(t // tm,),
        in_specs=[
            pl.BlockSpec((tm, TOK_WIDTH), lambda i: (i, 0)),
            pl.BlockSpec((tm, MEM_WIDTH), lambda i: (i, 0)),
            pl.BlockSpec((None, 1, MEM_LEN, 2 * MEM_WIDTH),
                         lambda i: (layer, i // per_batch, 0, 0)),
            pl.BlockSpec((tm, D_MODEL), lambda i: (i, 0)),
            _resident((D_MODEL, D_MODEL)),
            _resident((1, D_MODEL)),
        ] + c_in,
        out_specs=[pl.BlockSpec((tm, D_MODEL), lambda i: (i, 0))] + c_out,
        out_shape=[jax.ShapeDtypeStruct((t, D_MODEL), F32)] + c_shapes,
        compiler_params=_params(("parallel",)),
        name="mixer_tail",
    )(p1, qm, kvm, h, w_out_bf, gain.reshape(1, D_MODEL), *cast)


def _ffn_kernel(h_ref, gin_ref, wg_ref, wu_ref, wd_ref, gout_ref, o_ref, hn_scr):
    k = pl.program_id(1)

    def row_chunks(body):
        def step(c, carry):
            body(pl.ds(pl.multiple_of(c * ROW_CHUNK, ROW_CHUNK), ROW_CHUNK))
            return carry
        lax.fori_loop(0, h_ref.shape[0] // ROW_CHUNK, step, 0)

    @pl.when(k == 0)
    def _():
        def prologue(rows):
            hn_scr[rows, :] = _rms(h_ref[rows, :], gin_ref[...]).astype(BF16)
            o_ref[rows, :] = jnp.zeros((ROW_CHUNK, D_MODEL), F32)
        row_chunks(prologue)

    hn = hn_scr[...]
    gate = _dot(hn, wg_ref[...])
    up = _dot(hn, wu_ref[...])
    act = (gate * (1.0 / (1.0 + jnp.exp(-gate))) * up).astype(BF16)
    o_ref[...] += _dot(act, wd_ref[...])

    @pl.when(k == pl.num_programs(1) - 1)
    def _():
        def epilogue(rows):
            o_ref[rows, :] = h_ref[rows, :] + _rms(o_ref[rows, :], gout_ref[...])
        row_chunks(epilogue)


def _ffn(h, g_in, wg_bf, wu_bf, wd_bf, g_out, tm, tk):
    t = h.shape[0]
    d_ff = wg_bf.shape[1]
    return pl.pallas_call(
        _ffn_kernel,
        grid=(t // tm, d_ff // tk),
        in_specs=[
            pl.BlockSpec((tm, D_MODEL), lambda i, k: (i, 0)),
            _resident((1, D_MODEL)),
            pl.BlockSpec((D_MODEL, tk), lambda i, k: (0, k)),
            pl.BlockSpec((D_MODEL, tk), lambda i, k: (0, k)),
            pl.BlockSpec((tk, D_MODEL), lambda i, k: (k, 0)),
            _resident((1, D_MODEL)),
        ],
        out_specs=pl.BlockSpec((tm, D_MODEL), lambda i, k: (i, 0)),
        out_shape=jax.ShapeDtypeStruct((t, D_MODEL), F32),
        scratch_shapes=[pltpu.VMEM((tm, D_MODEL), BF16)],
        compiler_params=_params(("parallel", "arbitrary")),
        name="swiglu_ffn",
    )(h, g_in.reshape(1, D_MODEL), wg_bf, wu_bf, wd_bf, g_out.reshape(1, D_MODEL))


def _rope_group(x, cos, sin_signed, first_half):
    swapped = jnp.where(first_half, pltpu.roll(x, 96, 1), pltpu.roll(x, 32, 1))
    return x * cos + swapped * sin_signed


def _b_prep_kernel(h_ref, pos_ref, invf_ref, gk_ref, gq_ref, gc_ref, gqn_ref,
                   wkva_ref, wuk_ref, wuv_ref, win_ref, wuq_ref,
                   k_ref, v_ref, q_ref, qm_ref):
    tm = h_ref.shape[0]
    x = h_ref[...]
    y = x * lax.rsqrt(jnp.mean(x * x, axis=-1, keepdims=True) + EPS)
    hk = (y * gk_ref[...]).astype(BF16)
    hq = (y * gq_ref[...]).astype(BF16)

    ang = pos_ref[...].astype(F32) * invf_ref[...]
    lane = lax.broadcasted_iota(jnp.int32, (tm, LANES), 1)
    first_half = (lane % QK_ROPE) < (QK_ROPE // 2)
    cos = jnp.cos(ang)
    sin = jnp.sin(ang)
    sin_signed = jnp.where(first_half, -sin, sin)

    a = _dot(hk, wkva_ref[...])
    ckv = _rms(a[:, 0:KV_LORA], gc_ref[...]).astype(BF16)
    k_rope = _rope_group(a[:, KV_LORA:KVA_PAD], cos, sin_signed, first_half).astype(BF16)
    k_nope = _dot(ckv, wuk_ref[...])
    v_ref[...] = _dot(ckv, wuv_ref[...]).astype(BF16)

    z = _dot(hq, win_ref[...])
    qm_ref[...] = z[:, Q_LORA:].astype(BF16)
    cq = _rms(z[:, 0:Q_LORA], gqn_ref[...]).astype(BF16)
    q = _dot(cq, wuq_ref[...])
    for hh in range(MLA_HEADS):
        lo = hh * QK_PAD
        k_ref[:, lo:lo + LANES] = k_nope[:, hh * QK_NOPE:(hh + 1) * QK_NOPE].astype(BF16)
        k_ref[:, lo + LANES:lo + QK_PAD] = k_rope
        q_ref[:, lo:lo + LANES] = (q[:, lo:lo + LANES] * Q_SCALE).astype(BF16)
        q_rot = _rope_group(q[:, lo + LANES:lo + QK_PAD], cos, sin_signed, first_half)
        q_ref[:, lo + LANES:lo + QK_PAD] = (q_rot * Q_SCALE).astype(BF16)


def _b_prep(h, pos_col, inv_freq, g_kv, g_q, g_ckv, g_cq, wkva_bf, wuk_bf, wuv_bf,
            win_bf, wuq_bf, tm):
    t = h.shape[0]
    return pl.pallas_call(
        _b_prep_kernel,
        grid=(t // tm,),
        in_specs=[
            pl.BlockSpec((tm, D_MODEL), lambda i: (i, 0)),
            pl.BlockSpec((tm, 1), lambda i: (i, 0)),
            _resident((1, LANES)),
            _resident((1, D_MODEL)),
            _resident((1, D_MODEL)),
            _resident((1, KV_LORA)),
            _resident((1, Q_LORA)),
            _resident(wkva_bf.shape),
            _resident(wuk_bf.shape),
            _resident(wuv_bf.shape),
            _resident(win_bf.shape),
            _resident(wuq_bf.shape),
        ],
        out_specs=[
            pl.BlockSpec((tm, MLA_HEADS * QK_PAD), lambda i: (i, 0)),
            pl.BlockSpec((tm, MLA_HEADS * V_DIM), lambda i: (i, 0)),
            pl.BlockSpec((tm, MLA_HEADS * QK_PAD), lambda i: (i, 0)),
            pl.BlockSpec((tm, MEM_WIDTH), lambda i: (i, 0)),
        ],
        out_shape=[
            jax.ShapeDtypeStruct((t, MLA_HEADS * QK_PAD), BF16),
            jax.ShapeDtypeStruct((t, MLA_HEADS * V_DIM), BF16),
            jax.ShapeDtypeStruct((t, MLA_HEADS * QK_PAD), BF16),
            jax.ShapeDtypeStruct((t, MEM_WIDTH), BF16),
        ],
        compiler_params=_params(("parallel",)),
        name="b_prep",
    )(h, pos_col, inv_freq, g_kv.reshape(1, D_MODEL), g_q.reshape(1, D_MODEL),
      g_ckv.reshape(1, KV_LORA), g_cq.reshape(1, Q_LORA),
      wkva_bf, wuk_bf, wuv_bf, win_bf, wuq_bf)


def _mla_kernel(q_ref, k_ref, v_ref, o_ref, vaug_scr, *, blk):
    s_len = q_ref.shape[1]
    n_blk = s_len // blk
    neg = jnp.finfo(F32).min
    vaug_scr[:, 0:V_DIM] = v_ref[0]
    lane = lax.broadcasted_iota(jnp.int32, (s_len, LANES), 1)
    vaug_scr[:, V_DIM:] = jnp.where(lane == 0, 1.0, 0.0).astype(BF16)
    row = lax.broadcasted_iota(jnp.int32, (blk, blk), 0)
    col = lax.broadcasted_iota(jnp.int32, (blk, blk), 1)
    causal = row >= col

    def scores(qi, j):
        return _dot_nt(q_ref[0, qi * blk:(qi + 1) * blk, :], k_ref[0, j * blk:(j + 1) * blk, :])

    def accumulate(pending, acc):
        qi, j, p, alpha = pending
        pv = _dot(p, vaug_scr[j * blk:(j + 1) * blk, :])
        acc = pv if j == 0 else alpha * acc + pv
        if j == qi:
            out = acc[:, 0:V_DIM] / acc[:, V_DIM:V_DIM + 1]
            o_ref[0, qi * blk:(qi + 1) * blk, :] = out.astype(BF16)
        return acc

    pairs = [(qi, j) for qi in range(n_blk) for j in range(qi + 1)]
    s_next = scores(*pairs[0])
    m = acc = pending = None
    for t, (qi, j) in enumerate(pairs):
        s = s_next
        if t + 1 < len(pairs):
            s_next = scores(*pairs[t + 1])
        if j == qi:
            s = jnp.where(causal, s, neg)
        s_max = jnp.max(s, axis=-1, keepdims=True)
        if j == 0:
            m_new, alpha = s_max, None
        else:
            m_new = jnp.maximum(m, s_max)
            alpha = jnp.exp2(m - m_new)
        p = jnp.exp2(s - m_new).astype(BF16)
        if pending is not None:
            acc = accumulate(pending, acc)
        pending, m = (qi, j, p, alpha), m_new
    accumulate(pending, acc)


def _mla_attn(q, k, v, blk):
    b, s, _ = q.shape
    kern = functools.partial(_mla_kernel, blk=blk)
    return pl.pallas_call(
        kern,
        grid=(b, MLA_HEADS),
        in_specs=[
            pl.BlockSpec((1, s, QK_PAD), lambda bi, hi: (bi, 0, hi)),
            pl.BlockSpec((1, s, QK_PAD), lambda bi, hi: (bi, 0, hi)),
            pl.BlockSpec((1, s, V_DIM), lambda bi, hi: (bi, 0, hi)),
        ],
        out_specs=pl.BlockSpec((1, s, V_DIM), lambda bi, hi: (bi, 0, hi)),
        out_shape=jax.ShapeDtypeStruct((b, s, MLA_HEADS * V_DIM), BF16),
        scratch_shapes=[pltpu.VMEM((s, 2 * V_DIM), BF16)],
        compiler_params=_params(("parallel", "parallel")),
        name="mla_attn",
    )(q, k, v)


def kernel(x, mem, positions, norm_gains, mem_norm, w_mem_kv, ffn_w_gate, ffn_w_up, ffn_w_down,
           a_w_in, a_ln_g, a_ln_b, a_w_s, a_b_s, a_w_out, kv_src_norm, w_kv_a, kv_norm, w_uk, w_uv,
           b_w_in, b_q_norm, b_w_uq, b_w_out):
    b, s, d = x.shape
    t = b * s
    depth = norm_gains.shape[0]
    n_a = a_w_in.shape[0]
    assert d == D_MODEL and depth == 2 and n_a == 1 and b_w_in.shape[0] == 1

    bf = lambda w: w.astype(BF16)
    wuq = b_w_uq[0].reshape(Q_LORA, MLA_HEADS, QK_NOPE + QK_ROPE)
    wuq = jnp.pad(wuq, ((0, 0), (0, 0), (0, QK_PAD - QK_NOPE - QK_ROPE)))
    wuq_bf = bf(wuq.reshape(Q_LORA, MLA_HEADS * QK_PAD))
    wkva_bf = bf(jnp.pad(w_kv_a, ((0, 0), (0, KVA_PAD - KV_LORA - QK_ROPE))))
    inv_freq = ROPE_THETA ** (-jnp.arange(0, QK_ROPE, 2, dtype=F32) / QK_ROPE)
    inv_freq = jnp.tile(inv_freq, LANES // (QK_ROPE // 2)).reshape(1, LANES)

    h = x.reshape(t, d)
    kvm = _mem_kv(mem.reshape(b * MEM_LEN, d), mem_norm, bf(w_mem_kv))
    kvm = kvm.reshape(depth, b, MEM_LEN, 2 * MEM_WIDTH)

    ffn_w = (ffn_w_gate, ffn_w_up, ffn_w_down)
    tok, qm, *ffn0_bf = _a_front(h, norm_gains[0, 0], bf(a_w_in[0]), a_ln_g[0], a_ln_b[0],
                                 a_w_s[0], a_b_s[0].T, cast=ffn_w, cast_layer=0, tm=512)
    h, *ffn1_bf = _tail(tok, qm, kvm, h, bf(a_w_out[0]), norm_gains[0, 1], cast=ffn_w,
                        cast_layer=1, layer=0, tm=512, seq=s)
    h = _ffn(h, norm_gains[0, 2], *ffn0_bf, norm_gains[0, 3], tm=1024, tk=512)

    k_cat, v_all, q_cat, qm = _b_prep(
        h, positions.reshape(t, 1), inv_freq, kv_src_norm, norm_gains[1, 0], kv_norm, b_q_norm[0],
        wkva_bf, bf(w_uk), bf(w_uv), bf(b_w_in[0]), wuq_bf, tm=512)
    att = _mla_attn(q_cat.reshape(b, s, -1), k_cat.reshape(b, s, -1), v_all.reshape(b, s, -1),
                    blk=512)
    h, = _tail(att.reshape(t, TOK_WIDTH), qm, kvm, h, bf(b_w_out[0]), norm_gains[1, 1], cast=(),
               cast_layer=0, layer=1, tm=512, seq=s)
    h = _ffn(h, norm_gains[1, 2], *ffn1_bf, norm_gains[1, 3], tm=1024, tk=512)
    return h.reshape(b, s, d)
```

```python
import functools

import jax
import jax.numpy as jnp
from jax import lax
from jax.experimental import pallas as pl
from jax.experimental.pallas import tpu as pltpu

D_MODEL = 2048
HEAD_DIM = 128
MEM_LEN = 256
MEM_HEADS = 4
MEM_WIDTH = MEM_HEADS * HEAD_DIM
TOK_WIDTH = D_MODEL - MEM_WIDTH
CHUNK = 128
SGU_GROUPS = TOK_WIDTH // HEAD_DIM
MLA_HEADS = TOK_WIDTH // HEAD_DIM
QK_NOPE = 128
QK_ROPE = 64
V_DIM = 128
Q_LORA = 512
KV_LORA = 512
ROPE_THETA = 10000.0
EPS = 1e-6
MLA_SCALE = (QK_NOPE + QK_ROPE) ** -0.5
MEM_SCALE = HEAD_DIM ** -0.5
LOG2_E = 1.4426950408889634
Q_SCALE = MLA_SCALE * LOG2_E

LANES = 128
QK_PAD = 2 * LANES
KVA_PAD = KV_LORA + LANES
VMEM_LIMIT = 56 * 1024 * 1024
ROW_CHUNK = 128
COL_CHUNK = 256
SUB_ROWS = 256

BF16 = jnp.bfloat16
F32 = jnp.float32


def _dot(a, b):
    return jnp.dot(a, b, preferred_element_type=F32)


def _dot_nt(a, b):
    return lax.dot_general(a, b, (((1,), (1,)), ((), ())), preferred_element_type=F32)


def _rms(x, g):
    return x * lax.rsqrt(jnp.mean(x * x, axis=-1, keepdims=True) + EPS) * g


def _gelu_tanh(x):
    c = 0.7978845608028654
    return x * (0.5 * (1.0 + jnp.tanh(c * (x + 0.044715 * (x * x * x)))))


def _resident(shape):
    nd = len(shape)
    return pl.BlockSpec(shape, lambda *_: (0,) * nd, pipeline_mode=pl.Buffered(1))


def _emit_pipelined(stages):
    pending = None
    for matmul, consume in stages:
        value = matmul()
        if pending is not None:
            pending[1](pending[0])
        pending = (value, consume)
    pending[1](pending[0])


def _interleave(*stage_lists):
    return [stage for group in zip(*stage_lists) for stage in group]


def _params(sem):
    return pltpu.CompilerParams(dimension_semantics=sem, vmem_limit_bytes=VMEM_LIMIT)


def _mem_kv_kernel(mem_ref, g_ref, w_ref, o_ref):
    mn = _rms(mem_ref[...], g_ref[0]).astype(BF16)
    o_ref[0] = _dot(mn, w_ref[0]).astype(BF16)


def _mem_kv(mem2d, mem_norm, w_mem_kv_bf):
    depth = w_mem_kv_bf.shape[0]
    rows = mem2d.shape[0]
    return pl.pallas_call(
        _mem_kv_kernel,
        grid=(depth,),
        in_specs=[
            pl.BlockSpec((rows, D_MODEL), lambda l: (0, 0)),
            pl.BlockSpec((1, 1, D_MODEL), lambda l: (l, 0, 0)),
            pl.BlockSpec((1, D_MODEL, 2 * MEM_WIDTH), lambda l: (l, 0, 0)),
        ],
        out_specs=pl.BlockSpec((1, rows, 2 * MEM_WIDTH), lambda l: (l, 0, 0)),
        out_shape=jax.ShapeDtypeStruct((depth, rows, 2 * MEM_WIDTH), BF16),
        compiler_params=_params(("arbitrary",)),
        name="mem_kv",
    )(mem2d, mem_norm.reshape(depth, 1, D_MODEL), w_mem_kv_bf)


def _cast_specs(cast, steps):
    in_specs, out_specs, out_shapes = [], [], []
    for w, layer in cast:
        _, rows, cols = w.shape
        slab = rows // steps
        assert slab * steps == rows and slab % 16 == 0
        in_specs.append(pl.BlockSpec((None, slab, cols), lambda i, layer=layer: (layer, i, 0)))
        out_specs.append(pl.BlockSpec((slab, cols), lambda i: (i, 0)))
        out_shapes.append(jax.ShapeDtypeStruct((rows, cols), BF16))
    return in_specs, out_specs, out_shapes


def _cast_slabs(src_refs, dst_refs):
    for src, dst in zip(src_refs, dst_refs):
        dst[...] = src[...].astype(BF16)


def _a_front_kernel(x_ref, g_ref, w_ref, lng_ref, lnb_ref, ws_ref, bst_ref, *rest, n_cast):
    cast_in, (tok_ref, qm_ref), rest = rest[:n_cast], rest[n_cast:n_cast + 2], rest[n_cast + 2:]
    cast_out, (u_scr, vn_scr) = rest[:n_cast], rest[n_cast:]
    _cast_slabs(cast_in, cast_out)
    tm = x_ref.shape[0]
    hn = _rms(x_ref[...], g_ref[...]).astype(BF16)
    u_scr[...] = _gelu_tanh(_dot(hn, w_ref[:, 0:TOK_WIDTH]))
    v = _gelu_tanh(_dot(hn, w_ref[:, TOK_WIDTH:2 * TOK_WIDTH]))
    qm_ref[...] = _dot(hn, w_ref[:, 2 * TOK_WIDTH:]).astype(BF16)
    mu = jnp.mean(v, axis=-1, keepdims=True)
    vc = v - mu
    vn = vc * lax.rsqrt(jnp.mean(vc * vc, axis=-1, keepdims=True) + EPS)
    vn_scr[...] = (vn * lng_ref[...] + lnb_ref[...]).astype(BF16)

    n_chunks = tm // CHUNK
    row = lax.broadcasted_iota(jnp.int32, (CHUNK, CHUNK), 0)
    col = lax.broadcasted_iota(jnp.int32, (CHUNK, CHUNK), 1)
    causal = row >= col
    for g in range(SGU_GROUPS):
        cols = slice(g * HEAD_DIM, (g + 1) * HEAD_DIM)
        wg = jnp.where(causal, ws_ref[g], 0.0).astype(BF16)
        rhs = jnp.concatenate(
            [vn_scr[c * CHUNK:(c + 1) * CHUNK, cols] for c in range(n_chunks)], axis=1)
        s = _dot(wg, rhs) + bst_ref[:, g:g + 1]
        for c in range(n_chunks):
            rows = slice(c * CHUNK, (c + 1) * CHUNK)
            tok_ref[rows, cols] = (u_scr[rows, cols] * s[:, c * CHUNK:(c + 1) * CHUNK]).astype(BF16)


def _a_front(h, gain, w_in_bf, ln_g, ln_b, w_s, b_s_t, cast, tm):
    t = h.shape[0]
    a_in = w_in_bf.shape[1]
    c_in, c_out, c_shapes = _cast_specs(cast, t // tm)
    return pl.pallas_call(
        functools.partial(_a_front_kernel, n_cast=len(cast)),
        grid=(t // tm,),
        in_specs=[
            pl.BlockSpec((tm, D_MODEL), lambda i: (i, 0)),
            _resident((1, D_MODEL)),
            _resident((D_MODEL, a_in)),
            _resident((1, TOK_WIDTH)),
            _resident((1, TOK_WIDTH)),
            _resident((SGU_GROUPS, CHUNK, CHUNK)),
            _resident((CHUNK, SGU_GROUPS)),
        ] + c_in,
        out_specs=[
            pl.BlockSpec((tm, TOK_WIDTH), lambda i: (i, 0)),
            pl.BlockSpec((tm, MEM_WIDTH), lambda i: (i, 0)),
        ] + c_out,
        out_shape=[
            jax.ShapeDtypeStruct((t, TOK_WIDTH), BF16),
            jax.ShapeDtypeStruct((t, MEM_WIDTH), BF16),
        ] + c_shapes,
        scratch_shapes=[
            pltpu.VMEM((tm, TOK_WIDTH), F32),
            pltpu.VMEM((tm, TOK_WIDTH), BF16),
        ],
        compiler_params=_params(("parallel",)),
        name="a_front",
    )(h, gain.reshape(1, D_MODEL), w_in_bf, ln_g.reshape(1, TOK_WIDTH),
      ln_b.reshape(1, TOK_WIDTH), w_s, b_s_t, *[w for w, _ in cast])


def _tail_kernel(p1_ref, qm_ref, kv_ref, h_ref, w_ref, g_ref, *rest, n_cast):
    cast_in, o_ref, cast_out = rest[:n_cast], rest[n_cast], rest[n_cast + 1:]
    _cast_slabs(cast_in, cast_out)
    mix = _dot(p1_ref[...], w_ref[0:TOK_WIDTH, :])
    heads = []
    for hh in range(MEM_HEADS):
        cols = slice(hh * HEAD_DIM, (hh + 1) * HEAD_DIM)
        q = qm_ref[:, cols]
        k = kv_ref[0, :, cols]
        v = kv_ref[0, :, MEM_WIDTH + hh * HEAD_DIM:MEM_WIDTH + (hh + 1) * HEAD_DIM]
        s = _dot_nt(q, k) * MEM_SCALE
        p = jnp.exp(s - jnp.max(s, axis=-1, keepdims=True))
        l = jnp.sum(p, axis=-1, keepdims=True)
        heads.append((_dot(p.astype(BF16), v) / l).astype(BF16))
    mix = mix + _dot(jnp.concatenate(heads, axis=1), w_ref[TOK_WIDTH:, :])
    o_ref[...] = h_ref[...] + _rms(mix, g_ref[...])


def _tail(p1, qm, kvm, h, w_out_bf, gain, cast, layer, tm, seq):
    t = h.shape[0]
    per_batch = seq // tm
    c_in, c_out, c_shapes = _cast_specs(cast, t // tm)
    return pl.pallas_call(
        functools.partial(_tail_kernel, n_cast=len(cast)),
        grid=(t // tm,),
        in_specs=[
            pl.BlockSpec((tm, TOK_WIDTH), lambda i: (i, 0)),
            pl.BlockSpec((tm, MEM_WIDTH), lambda i: (i, 0)),
            pl.BlockSpec((None, 1, MEM_LEN, 2 * MEM_WIDTH),
                         lambda i: (layer, i // per_batch, 0, 0)),
            pl.BlockSpec((tm, D_MODEL), lambda i: (i, 0)),
            _resident((D_MODEL, D_MODEL)),
            _resident((1, D_MODEL)),
        ] + c_in,
        out_specs=[pl.BlockSpec((tm, D_MODEL), lambda i: (i, 0))] + c_out,
        out_shape=[jax.ShapeDtypeStruct((t, D_MODEL), F32)] + c_shapes,
        compiler_params=_params(("parallel",)),
        name="mixer_tail",
    )(p1, qm, kvm, h, w_out_bf, gain.reshape(1, D_MODEL), *[w for w, _ in cast])


def _ffn_kernel(h_ref, gin_ref, wg_ref, wu_ref, wd_ref, gout_ref, o_ref, hn_scr):
    k = pl.program_id(1)

    def row_chunks(body):
        def step(c, carry):
            body(pl.ds(pl.multiple_of(c * ROW_CHUNK, ROW_CHUNK), ROW_CHUNK))
            return carry
        lax.fori_loop(0, h_ref.shape[0] // ROW_CHUNK, step, 0, unroll=2)

    @pl.when(k == 0)
    def _():
        def prologue(rows):
            hn_scr[rows, :] = _rms(h_ref[rows, :], gin_ref[...]).astype(BF16)
            o_ref[rows, :] = jnp.zeros((ROW_CHUNK, D_MODEL), F32)
        row_chunks(prologue)

    hn = hn_scr[...]
    gate = _dot(hn, wg_ref[...])
    up = _dot(hn, wu_ref[...])
    act = (gate * (1.0 / (1.0 + jnp.exp(-gate))) * up).astype(BF16)
    o_ref[...] += _dot(act, wd_ref[...])

    @pl.when(k == pl.num_programs(1) - 1)
    def _():
        def epilogue(rows):
            o_ref[rows, :] = h_ref[rows, :] + _rms(o_ref[rows, :], gout_ref[...])
        row_chunks(epilogue)


def _ffn(h, g_in, wg_bf, wu_bf, wd_bf, g_out, tm, tk):
    t = h.shape[0]
    d_ff = wg_bf.shape[1]
    return pl.pallas_call(
        _ffn_kernel,
        grid=(t // tm, d_ff // tk),
        in_specs=[
            pl.BlockSpec((tm, D_MODEL), lambda i, k: (i, 0)),
            _resident((1, D_MODEL)),
            pl.BlockSpec((D_MODEL, tk), lambda i, k: (0, k)),
            pl.BlockSpec((D_MODEL, tk), lambda i, k: (0, k)),
            pl.BlockSpec((tk, D_MODEL), lambda i, k: (k, 0)),
            _resident((1, D_MODEL)),
        ],
        out_specs=pl.BlockSpec((tm, D_MODEL), lambda i, k: (i, 0)),
        out_shape=jax.ShapeDtypeStruct((t, D_MODEL), F32),
        scratch_shapes=[pltpu.VMEM((tm, D_MODEL), BF16)],
        compiler_params=_params(("parallel", "arbitrary")),
        name="swiglu_ffn",
    )(h, g_in.reshape(1, D_MODEL), wg_bf, wu_bf, wd_bf, g_out.reshape(1, D_MODEL))


def _rope_group(x, cos, sin_signed, first_half):
    swapped = jnp.where(first_half, pltpu.roll(x, 96, 1), pltpu.roll(x, 32, 1))
    return x * cos + swapped * sin_signed


def _b_prep_kernel(h_ref, pos_ref, invf_ref, gk_ref, gq_ref, gc_ref, gqn_ref,
                   wkva_ref, wuk_ref, wuv_ref, win_ref, wuq_ref,
                   k_ref, v_ref, q_ref, qm_ref):
    subs = [slice(lo, lo + SUB_ROWS) for lo in range(0, h_ref.shape[0], SUB_ROWS)]
    lane = lax.broadcasted_iota(jnp.int32, (SUB_ROWS, LANES), 1)
    first_half = (lane % QK_ROPE) < (QK_ROPE // 2)

    def rope_tables(rows):
        ang = pos_ref[rows, :].astype(F32) * invf_ref[...]
        sin = jnp.sin(ang)
        return jnp.cos(ang), jnp.where(first_half, -sin, sin)

    def sub_block(rows):
        state = {}

        def in_projections():
            x = h_ref[rows, :]
            y = x * lax.rsqrt(jnp.mean(x * x, axis=-1, keepdims=True) + EPS)
            hk = (y * gk_ref[...]).astype(BF16)
            hq = (y * gq_ref[...]).astype(BF16)
            return _dot(hk, wkva_ref[...]), _dot(hq, win_ref[...])

        def latents(az):
            a, z = az
            cos, sin_signed = state["rope"] = rope_tables(rows)
            state["ckv"] = _rms(a[:, 0:KV_LORA], gc_ref[...]).astype(BF16)
            state["k_rope"] = _rope_group(
                a[:, KV_LORA:KVA_PAD], cos, sin_signed, first_half).astype(BF16)
            state["cq"] = _rms(z[:, 0:Q_LORA], gqn_ref[...]).astype(BF16)
            qm_ref[rows, :] = z[:, Q_LORA:].astype(BF16)

        def values():
            return _dot(state["ckv"], wuv_ref[...])

        def store_values(v):
            v_ref[rows, :] = v.astype(BF16)

        def k_chunk(lo):
            def matmul():
                return _dot(state["ckv"], wuk_ref[:, lo:lo + COL_CHUNK])

            def consume(k_nope):
                for c in range(0, COL_CHUNK, QK_NOPE):
                    dst = (lo + c) // QK_NOPE * QK_PAD
                    k_ref[rows, dst:dst + LANES] = k_nope[:, c:c + QK_NOPE].astype(BF16)
                    k_ref[rows, dst + LANES:dst + QK_PAD] = state["k_rope"]
            return matmul, consume

        def q_chunk(lo):
            def matmul():
                return _dot(state["cq"], wuq_ref[:, lo:lo + COL_CHUNK])

            def consume(q):
                cos, sin_signed = state["rope"]
                for c in range(0, COL_CHUNK, QK_PAD):
                    q_ref[rows, lo + c:lo + c + LANES] = (q[:, c:c + LANES] * Q_SCALE).astype(BF16)
                    q_rot = _rope_group(q[:, c + LANES:c + QK_PAD], cos, sin_signed, first_half)
                    q_ref[rows, lo + c + LANES:lo + c + QK_PAD] = (q_rot * Q_SCALE).astype(BF16)
            return matmul, consume

        stages = [(in_projections, latents), (values, store_values)]
        stages += [k_chunk(lo) for lo in range(0, MLA_HEADS * QK_NOPE, COL_CHUNK)]
        stages += [q_chunk(lo) for lo in range(0, MLA_HEADS * QK_PAD, COL_CHUNK)]
        return stages

    assert len(subs) >= 2
    _emit_pipelined(_interleave(*[sub_block(rows) for rows in subs]))


def _b_prep(h, pos_col, inv_freq, g_kv, g_q, g_ckv, g_cq, wkva_bf, wuk_bf, wuv_bf,
            win_bf, wuq_bf, tm):
    t = h.shape[0]
    return pl.pallas_call(
        _b_prep_kernel,
        grid=(t // tm,),
        in_specs=[
            pl.BlockSpec((tm, D_MODEL), lambda i: (i, 0)),
            pl.BlockSpec((tm, 1), lambda i: (i, 0)),
            _resident((1, LANES)),
            _resident((1, D_MODEL)),
            _resident((1, D_MODEL)),
            _resident((1, KV_LORA)),
            _resident((1, Q_LORA)),
            _resident(wkva_bf.shape),
            _resident(wuk_bf.shape),
            _resident(wuv_bf.shape),
            _resident(win_bf.shape),
            _resident(wuq_bf.shape),
        ],
        out_specs=[
            pl.BlockSpec((tm, MLA_HEADS * QK_PAD), lambda i: (i, 0)),
            pl.BlockSpec((tm, MLA_HEADS * V_DIM), lambda i: (i, 0)),
            pl.BlockSpec((tm, MLA_HEADS * QK_PAD), lambda i: (i, 0)),
            pl.BlockSpec((tm, MEM_WIDTH), lambda i: (i, 0)),
        ],
        out_shape=[
            jax.ShapeDtypeStruct((t, MLA_HEADS * QK_PAD), BF16),
            jax.ShapeDtypeStruct((t, MLA_HEADS * V_DIM), BF16),
            jax.ShapeDtypeStruct((t, MLA_HEADS * QK_PAD), BF16),
            jax.ShapeDtypeStruct((t, MEM_WIDTH), BF16),
        ],
        compiler_params=_params(("parallel",)),
        name="b_prep",
    )(h, pos_col, inv_freq, g_kv.reshape(1, D_MODEL), g_q.reshape(1, D_MODEL),
      g_ckv.reshape(1, KV_LORA), g_cq.reshape(1, Q_LORA),
      wkva_bf, wuk_bf, wuv_bf, win_bf, wuq_bf)


def _mla_kernel(q_ref, k_ref, v_ref, o_ref, vaug_scr, *, blk):
    s_len = q_ref.shape[1]
    n_blk = s_len // blk
    neg = jnp.finfo(F32).min
    vaug_scr[:, 0:V_DIM] = v_ref[0]
    lane = lax.broadcasted_iota(jnp.int32, (s_len, LANES), 1)
    vaug_scr[:, V_DIM:] = jnp.where(lane == 0, 1.0, 0.0).astype(BF16)
    row = lax.broadcasted_iota(jnp.int32, (blk, blk), 0)
    col = lax.broadcasted_iota(jnp.int32, (blk, blk), 1)
    causal = row >= col

    def scores(qi, j):
        return _dot_nt(q_ref[0, qi * blk:(qi + 1) * blk, :], k_ref[0, j * blk:(j + 1) * blk, :])

    def accumulate(pending, acc):
        qi, j, p, alpha = pending
        pv = _dot(p, vaug_scr[j * blk:(j + 1) * blk, :])
        acc = pv if j == 0 else alpha * acc + pv
        if j == qi:
            out = acc[:, 0:V_DIM] / acc[:, V_DIM:V_DIM + 1]
            o_ref[0, qi * blk:(qi + 1) * blk, :] = out.astype(BF16)
        return acc

    pairs = [(qi, j) for qi in range(n_blk) for j in range(qi + 1)]
    s_next = scores(*pairs[0])
    m = acc = pending = None
    for t, (qi, j) in enumerate(pairs):
        s = s_next
        if t + 1 < len(pairs):
            s_next = scores(*pairs[t + 1])
        if j == qi:
            s = jnp.where(causal, s, neg)
        s_max = jnp.max(s, axis=-1, keepdims=True)
        if j == 0:
            m_new, alpha = s_max, None
        else:
            m_new = jnp.maximum(m, s_max)
            alpha = jnp.exp2(m - m_new)
        p = jnp.exp2(s - m_new).astype(BF16)
        if pending is not None:
            acc = accumulate(pending, acc)
        pending, m = (qi, j, p, alpha), m_new
    accumulate(pending, acc)


def _mla_attn(q, k, v, blk):
    b, s, _ = q.shape
    kern = functools.partial(_mla_kernel, blk=blk)
    return pl.pallas_call(
        kern,
        grid=(b, MLA_HEADS),
        in_specs=[
            pl.BlockSpec((1, s, QK_PAD), lambda bi, hi: (bi, 0, hi)),
            pl.BlockSpec((1, s, QK_PAD), lambda bi, hi: (bi, 0, hi)),
            pl.BlockSpec((1, s, V_DIM), lambda bi, hi: (bi, 0, hi)),
        ],
        out_specs=pl.BlockSpec((1, s, V_DIM), lambda bi, hi: (bi, 0, hi)),
        out_shape=jax.ShapeDtypeStruct((b, s, MLA_HEADS * V_DIM), BF16),
        scratch_shapes=[pltpu.VMEM((s, 2 * V_DIM), BF16)],
        compiler_params=_params(("parallel", "parallel")),
        name="mla_attn",
    )(q, k, v)


def kernel(x, mem, positions, norm_gains, mem_norm, w_mem_kv, ffn_w_gate, ffn_w_up, ffn_w_down,
           a_w_in, a_ln_g, a_ln_b, a_w_s, a_b_s, a_w_out, kv_src_norm, w_kv_a, kv_norm, w_uk, w_uv,
           b_w_in, b_q_norm, b_w_uq, b_w_out):
    b, s, d = x.shape
    t = b * s
    depth = norm_gains.shape[0]
    n_a = a_w_in.shape[0]
    assert d == D_MODEL and depth == 2 and n_a == 1 and b_w_in.shape[0] == 1

    bf = lambda w: w.astype(BF16)
    wuq = b_w_uq[0].reshape(Q_LORA, MLA_HEADS, QK_NOPE + QK_ROPE)
    wuq = jnp.pad(wuq, ((0, 0), (0, 0), (0, QK_PAD - QK_NOPE - QK_ROPE)))
    wuq_bf = bf(wuq.reshape(Q_LORA, MLA_HEADS * QK_PAD))
    wkva_bf = bf(jnp.pad(w_kv_a, ((0, 0), (0, KVA_PAD - KV_LORA - QK_ROPE))))
    inv_freq = ROPE_THETA ** (-jnp.arange(0, QK_ROPE, 2, dtype=F32) / QK_ROPE)
    inv_freq = jnp.tile(inv_freq, LANES // (QK_ROPE // 2)).reshape(1, LANES)

    h = x.reshape(t, d)
    kvm = _mem_kv(mem.reshape(b * MEM_LEN, d), mem_norm, bf(w_mem_kv))
    kvm = kvm.reshape(depth, b, MEM_LEN, 2 * MEM_WIDTH)

    ffn_w = (ffn_w_gate, ffn_w_up, ffn_w_down)
    tok, qm, w_out0_bf, *ffn0_bf = _a_front(
        h, norm_gains[0, 0], bf(a_w_in[0]), a_ln_g[0], a_ln_b[0], a_w_s[0], a_b_s[0].T,
        cast=[(a_w_out, 0)] + [(w, 0) for w in ffn_w], tm=512)
    h, w_out1_bf, w_in1_bf, wuk_bf, wuv_bf, *ffn1_bf = _tail(
        tok, qm, kvm, h, w_out0_bf, norm_gains[0, 1],
        cast=[(b_w_out, 0), (b_w_in, 0), (w_uk[None], 0), (w_uv[None], 0)] + [(w, 1) for w in ffn_w],
        layer=0, tm=512, seq=s)
    h = _ffn(h, norm_gains[0, 2], *ffn0_bf, norm_gains[0, 3], tm=1024, tk=512)

    k_cat, v_all, q_cat, qm = _b_prep(
        h, positions.reshape(t, 1), inv_freq, kv_src_norm, norm_gains[1, 0], kv_norm, b_q_norm[0],
        wkva_bf, wuk_bf, wuv_bf, w_in1_bf, wuq_bf, tm=512)
    att = _mla_attn(q_cat.reshape(b, s, -1), k_cat.reshape(b, s, -1), v_all.reshape(b, s, -1),
                    blk=512)
    h, = _tail(att.reshape(t, TOK_WIDTH), qm, kvm, h, w_out1_bf, norm_gains[1, 1], cast=[],
               layer=1, tm=512, seq=s)
    h = _ffn(h, norm_gains[1, 2], *ffn1_bf, norm_gains[1, 3], tm=1024, tk=512)
    return h.reshape(b, s, d)
```

```python
import functools

import jax
import jax.numpy as jnp
from jax import lax
from jax.experimental import pallas as pl
from jax.experimental.pallas import tpu as pltpu

D_MODEL = 2048
HEAD_DIM = 128
MEM_LEN = 256
MEM_HEADS = 4
MEM_WIDTH = MEM_HEADS * HEAD_DIM
TOK_WIDTH = D_MODEL - MEM_WIDTH
CHUNK = 128
SGU_GROUPS = TOK_WIDTH // HEAD_DIM
MLA_HEADS = TOK_WIDTH // HEAD_DIM
QK_NOPE = 128
QK_ROPE = 64
V_DIM = 128
Q_LORA = 512
KV_LORA = 512
ROPE_THETA = 10000.0
EPS = 1e-6
MLA_SCALE = (QK_NOPE + QK_ROPE) ** -0.5
MEM_SCALE = HEAD_DIM ** -0.5
LOG2_E = 1.4426950408889634
Q_SCALE = MLA_SCALE * LOG2_E

LANES = 128
QK_PAD = 2 * LANES
KVA_PAD = KV_LORA + LANES
VMEM_LIMIT = 56 * 1024 * 1024
ROW_CHUNK = 128
COL_CHUNK = 256
SUB_ROWS = 256

BF16 = jnp.bfloat16
F32 = jnp.float32


def _dot(a, b):
    return jnp.dot(a, b, preferred_element_type=F32)


def _dot_nt(a, b):
    return lax.dot_general(a, b, (((1,), (1,)), ((), ())), preferred_element_type=F32)


def _rms(x, g):
    return x * lax.rsqrt(jnp.mean(x * x, axis=-1, keepdims=True) + EPS) * g


def _gelu_tanh(x):
    c = 0.7978845608028654
    return x * (0.5 * (1.0 + jnp.tanh(c * (x + 0.044715 * (x * x * x)))))


def _resident(shape):
    nd = len(shape)
    return pl.BlockSpec(shape, lambda *_: (0,) * nd, pipeline_mode=pl.Buffered(1))


def _emit_pipelined(stages):
    pending = None
    for matmul, consume in stages:
        value = matmul()
        if pending is not None:
            pending[1](pending[0])
        pending = (value, consume)
    pending[1](pending[0])


def _interleave(*stage_lists):
    return [stage for group in zip(*stage_lists) for stage in group]


def _params(sem):
    return pltpu.CompilerParams(dimension_semantics=sem, vmem_limit_bytes=VMEM_LIMIT)


def _mem_kv_kernel(mem_ref, g_ref, w_ref, o_ref):
    mn = _rms(mem_ref[...], g_ref[0]).astype(BF16)
    o_ref[0] = _dot(mn, w_ref[0].astype(BF16)).astype(BF16)


def _mem_kv(mem2d, mem_norm, w_mem_kv):
    depth = w_mem_kv.shape[0]
    rows = mem2d.shape[0]
    return pl.pallas_call(
        _mem_kv_kernel,
        grid=(depth,),
        in_specs=[
            pl.BlockSpec((rows, D_MODEL), lambda l: (0, 0)),
            pl.BlockSpec((1, 1, D_MODEL), lambda l: (l, 0, 0)),
            pl.BlockSpec((1, D_MODEL, 2 * MEM_WIDTH), lambda l: (l, 0, 0)),
        ],
        out_specs=pl.BlockSpec((1, rows, 2 * MEM_WIDTH), lambda l: (l, 0, 0)),
        out_shape=jax.ShapeDtypeStruct((depth, rows, 2 * MEM_WIDTH), BF16),
        compiler_params=_params(("arbitrary",)),
        name="mem_kv",
    )(mem2d, mem_norm.reshape(depth, 1, D_MODEL), w_mem_kv)


def _cast_specs(cast, steps):
    in_specs, out_specs, out_shapes = [], [], []
    for w, layer in cast:
        _, rows, cols = w.shape
        slab = rows // steps
        assert slab * steps == rows and slab % 16 == 0
        in_specs.append(pl.BlockSpec((None, slab, cols), lambda i, layer=layer: (layer, i, 0)))
        out_specs.append(pl.BlockSpec((slab, cols), lambda i: (i, 0)))
        out_shapes.append(jax.ShapeDtypeStruct((rows, cols), BF16))
    return in_specs, out_specs, out_shapes


def _cast_slabs(src_refs, dst_refs):
    for src, dst in zip(src_refs, dst_refs):
        dst[...] = src[...].astype(BF16)


def _a_front_kernel(x_ref, g_ref, w_ref, lng_ref, lnb_ref, ws_ref, bst_ref, *rest, n_cast):
    cast_in, (tok_ref, qm_ref), rest = rest[:n_cast], rest[n_cast:n_cast + 2], rest[n_cast + 2:]
    cast_out, (u_scr, vn_scr) = rest[:n_cast], rest[n_cast:]
    _cast_slabs(cast_in, cast_out)
    tm = x_ref.shape[0]
    hn = _rms(x_ref[...], g_ref[...]).astype(BF16)
    u_scr[...] = _gelu_tanh(_dot(hn, w_ref[:, 0:TOK_WIDTH]))
    v = _gelu_tanh(_dot(hn, w_ref[:, TOK_WIDTH:2 * TOK_WIDTH]))
    qm_ref[...] = _dot(hn, w_ref[:, 2 * TOK_WIDTH:]).astype(BF16)
    mu = jnp.mean(v, axis=-1, keepdims=True)
    vc = v - mu
    vn = vc * lax.rsqrt(jnp.mean(vc * vc, axis=-1, keepdims=True) + EPS)
    vn_scr[...] = (vn * lng_ref[...] + lnb_ref[...]).astype(BF16)

    n_chunks = tm // CHUNK
    row = lax.broadcasted_iota(jnp.int32, (CHUNK, CHUNK), 0)
    col = lax.broadcasted_iota(jnp.int32, (CHUNK, CHUNK), 1)
    causal = row >= col
    for g in range(SGU_GROUPS):
        cols = slice(g * HEAD_DIM, (g + 1) * HEAD_DIM)
        wg = jnp.where(causal, ws_ref[g], 0.0).astype(BF16)
        rhs = jnp.concatenate(
            [vn_scr[c * CHUNK:(c + 1) * CHUNK, cols] for c in range(n_chunks)], axis=1)
        s = _dot(wg, rhs) + bst_ref[:, g:g + 1]
        for c in range(n_chunks):
            rows = slice(c * CHUNK, (c + 1) * CHUNK)
            tok_ref[rows, cols] = (u_scr[rows, cols] * s[:, c * CHUNK:(c + 1) * CHUNK]).astype(BF16)


def _a_front(h, gain, w_in_bf, ln_g, ln_b, w_s, b_s_t, cast, tm):
    t = h.shape[0]
    a_in = w_in_bf.shape[1]
    c_in, c_out, c_shapes = _cast_specs(cast, t // tm)
    return pl.pallas_call(
        functools.partial(_a_front_kernel, n_cast=len(cast)),
        grid=(t // tm,),
        in_specs=[
            pl.BlockSpec((tm, D_MODEL), lambda i: (i, 0)),
            _resident((1, D_MODEL)),
            _resident((D_MODEL, a_in)),
            _resident((1, TOK_WIDTH)),
            _resident((1, TOK_WIDTH)),
            _resident((SGU_GROUPS, CHUNK, CHUNK)),
            _resident((CHUNK, SGU_GROUPS)),
        ] + c_in,
        out_specs=[
            pl.BlockSpec((tm, TOK_WIDTH), lambda i: (i, 0)),
            pl.BlockSpec((tm, MEM_WIDTH), lambda i: (i, 0)),
        ] + c_out,
        out_shape=[
            jax.ShapeDtypeStruct((t, TOK_WIDTH), BF16),
            jax.ShapeDtypeStruct((t, MEM_WIDTH), BF16),
        ] + c_shapes,
        scratch_shapes=[
            pltpu.VMEM((tm, TOK_WIDTH), F32),
            pltpu.VMEM((tm, TOK_WIDTH), BF16),
        ],
        compiler_params=_params(("parallel",)),
        name="a_front",
    )(h, gain.reshape(1, D_MODEL), w_in_bf, ln_g.reshape(1, TOK_WIDTH),
      ln_b.reshape(1, TOK_WIDTH), w_s, b_s_t, *[w for w, _ in cast])


def _tail_kernel(p1_ref, qm_ref, kv_ref, h_ref, w_ref, g_ref, *rest, n_cast):
    cast_in, o_ref, cast_out = rest[:n_cast], rest[n_cast], rest[n_cast + 1:]
    _cast_slabs(cast_in, cast_out)
    mix = _dot(p1_ref[...], w_ref[0:TOK_WIDTH, :])
    heads = []
    for hh in range(MEM_HEADS):
        cols = slice(hh * HEAD_DIM, (hh + 1) * HEAD_DIM)
        q = qm_ref[:, cols]
        k = kv_ref[0, :, cols]
        v = kv_ref[0, :, MEM_WIDTH + hh * HEAD_DIM:MEM_WIDTH + (hh + 1) * HEAD_DIM]
        s = _dot_nt(q, k) * MEM_SCALE
        p = jnp.exp(s - jnp.max(s, axis=-1, keepdims=True))
        l = jnp.sum(p, axis=-1, keepdims=True)
        heads.append((_dot(p.astype(BF16), v) / l).astype(BF16))
    mix = mix + _dot(jnp.concatenate(heads, axis=1), w_ref[TOK_WIDTH:, :])
    o_ref[...] = h_ref[...] + _rms(mix, g_ref[...])


def _tail(p1, qm, kvm, h, w_out_bf, gain, cast, layer, tm, seq):
    t = h.shape[0]
    per_batch = seq // tm
    c_in, c_out, c_shapes = _cast_specs(cast, t // tm)
    return pl.pallas_call(
        functools.partial(_tail_kernel, n_cast=len(cast)),
        grid=(t // tm,),
        in_specs=[
            pl.BlockSpec((tm, TOK_WIDTH), lambda i: (i, 0)),
            pl.BlockSpec((tm, MEM_WIDTH), lambda i: (i, 0)),
            pl.BlockSpec((None, 1, MEM_LEN, 2 * MEM_WIDTH),
                         lambda i: (layer, i // per_batch, 0, 0)),
            pl.BlockSpec((tm, D_MODEL), lambda i: (i, 0)),
            _resident((D_MODEL, D_MODEL)),
            _resident((1, D_MODEL)),
        ] + c_in,
        out_specs=[pl.BlockSpec((tm, D_MODEL), lambda i: (i, 0))] + c_out,
        out_shape=[jax.ShapeDtypeStruct((t, D_MODEL), F32)] + c_shapes,
        compiler_params=_params(("parallel",)),
        name="mixer_tail",
    )(p1, qm, kvm, h, w_out_bf, gain.reshape(1, D_MODEL), *[w for w, _ in cast])


def _ffn_kernel(h_ref, gin_ref, wg_ref, wu_ref, wd_ref, gout_ref, o_ref, hn_scr):
    k = pl.program_id(1)
    last = pl.num_programs(1) - 1
    slabs = [slice(lo, lo + ROW_CHUNK) for lo in range(0, h_ref.shape[0], ROW_CHUNK)]

    def step(is_first, is_last):
        if is_first:
            for rows in slabs:
                hn_scr[rows, :] = _rms(h_ref[rows, :], gin_ref[...]).astype(BF16)
        hn = hn_scr[...]
        gate = _dot(hn, wg_ref[...])
        up = _dot(hn, wu_ref[...])
        act = (gate * (1.0 / (1.0 + jnp.exp(-gate))) * up).astype(BF16)
        part = _dot(act, wd_ref[...])
        if is_first:
            o_ref[...] = part
        else:
            o_ref[...] += part
        if is_last:
            for rows in slabs:
                o_ref[rows, :] = h_ref[rows, :] + _rms(o_ref[rows, :], gout_ref[...])

    pl.when(k == 0)(lambda: step(True, False))
    pl.when(jnp.logical_and(k > 0, k < last))(lambda: step(False, False))
    pl.when(k == last)(lambda: step(False, True))


def _ffn(h, g_in, wg_bf, wu_bf, wd_bf, g_out, tm, tk):
    t = h.shape[0]
    d_ff = wg_bf.shape[1]
    return pl.pallas_call(
        _ffn_kernel,
        grid=(t // tm, d_ff // tk),
        in_specs=[
            pl.BlockSpec((tm, D_MODEL), lambda i, k: (i, 0)),
            _resident((1, D_MODEL)),
            pl.BlockSpec((D_MODEL, tk), lambda i, k: (0, k)),
            pl.BlockSpec((D_MODEL, tk), lambda i, k: (0, k)),
            pl.BlockSpec((tk, D_MODEL), lambda i, k: (k, 0)),
            _resident((1, D_MODEL)),
        ],
        out_specs=pl.BlockSpec((tm, D_MODEL), lambda i, k: (i, 0)),
        out_shape=jax.ShapeDtypeStruct((t, D_MODEL), F32),
        scratch_shapes=[pltpu.VMEM((tm, D_MODEL), BF16)],
        compiler_params=_params(("parallel", "arbitrary")),
        name="swiglu_ffn",
    )(h, g_in.reshape(1, D_MODEL), wg_bf, wu_bf, wd_bf, g_out.reshape(1, D_MODEL))


def _rope_group(x, cos, sin_signed, first_half):
    swapped = jnp.where(first_half, pltpu.roll(x, 96, 1), pltpu.roll(x, 32, 1))
    return x * cos + swapped * sin_signed


def _b_prep_kernel(h_ref, pos_ref, invf_ref, gk_ref, gq_ref, gc_ref, gqn_ref,
                   wkva_ref, wuk_ref, wuv_ref, win_ref, wuq_ref,
                   k_ref, v_ref, q_ref, qm_ref):
    subs = [slice(lo, lo + SUB_ROWS) for lo in range(0, h_ref.shape[0], SUB_ROWS)]
    lane = lax.broadcasted_iota(jnp.int32, (SUB_ROWS, LANES), 1)
    first_half = (lane % QK_ROPE) < (QK_ROPE // 2)

    def rope_tables(rows):
        ang = pos_ref[rows, :].astype(F32) * invf_ref[...]
        sin = jnp.sin(ang)
        return jnp.cos(ang), jnp.where(first_half, -sin, sin)

    def sub_block(rows):
        state = {}

        def in_projections():
            x = h_ref[rows, :]
            y = x * lax.rsqrt(jnp.mean(x * x, axis=-1, keepdims=True) + EPS)
            hk = (y * gk_ref[...]).astype(BF16)
            hq = (y * gq_ref[...]).astype(BF16)
            return _dot(hk, wkva_ref[...]), _dot(hq, win_ref[...])

        def latents(az):
            a, z = az
            cos, sin_signed = state["rope"] = rope_tables(rows)
            state["ckv"] = _rms(a[:, 0:KV_LORA], gc_ref[...]).astype(BF16)
            state["k_rope"] = _rope_group(
                a[:, KV_LORA:KVA_PAD], cos, sin_signed, first_half).astype(BF16)
            state["cq"] = _rms(z[:, 0:Q_LORA], gqn_ref[...]).astype(BF16)
            qm_ref[rows, :] = z[:, Q_LORA:].astype(BF16)

        def values():
            return _dot(state["ckv"], wuv_ref[...])

        def store_values(v):
            v_ref[rows, :] = v.astype(BF16)

        def k_chunk(lo):
            def matmul():
                return _dot(state["ckv"], wuk_ref[:, lo:lo + COL_CHUNK])

            def consume(k_nope):
                for c in range(0, COL_CHUNK, QK_NOPE):
                    dst = (lo + c) // QK_NOPE * QK_PAD
                    k_ref[rows, dst:dst + LANES] = k_nope[:, c:c + QK_NOPE].astype(BF16)
                    k_ref[rows, dst + LANES:dst + QK_PAD] = state["k_rope"]
            return matmul, consume

        def q_chunk(lo):
            def matmul():
                return _dot(state["cq"], wuq_ref[:, lo:lo + COL_CHUNK])

            def consume(q):
                cos, sin_signed = state["rope"]
                for c in range(0, COL_CHUNK, QK_PAD):
                    q_ref[rows, lo + c:lo + c + LANES] = (q[:, c:c + LANES] * Q_SCALE).astype(BF16)
                    q_rot = _rope_group(q[:, c + LANES:c + QK_PAD], cos, sin_signed, first_half)
                    q_ref[rows, lo + c + LANES:lo + c + QK_PAD] = (q_rot * Q_SCALE).astype(BF16)
            return matmul, consume

        stages = [(in_projections, latents), (values, store_values)]
        stages += [k_chunk(lo) for lo in range(0, MLA_HEADS * QK_NOPE, COL_CHUNK)]
        stages += [q_chunk(lo) for lo in range(0, MLA_HEADS * QK_PAD, COL_CHUNK)]
        return stages

    assert len(subs) >= 2
    _emit_pipelined(_interleave(*[sub_block(rows) for rows in subs]))


def _b_prep(h, pos_col, inv_freq, g_kv, g_q, g_ckv, g_cq, wkva_bf, wuk_bf, wuv_bf,
            win_bf, wuq_bf, tm):
    t = h.shape[0]
    return pl.pallas_call(
        _b_prep_kernel,
        grid=(t // tm,),
        in_specs=[
            pl.BlockSpec((tm, D_MODEL), lambda i: (i, 0)),
            pl.BlockSpec((tm, 1), lambda i: (i, 0)),
            _resident((1, LANES)),
            _resident((1, D_MODEL)),
            _resident((1, D_MODEL)),
            _resident((1, KV_LORA)),
            _resident((1, Q_LORA)),
            _resident(wkva_bf.shape),
            _resident(wuk_bf.shape),
            _resident(wuv_bf.shape),
            _resident(win_bf.shape),
            _resident(wuq_bf.shape),
        ],
        out_specs=[
            pl.BlockSpec((tm, MLA_HEADS * QK_PAD), lambda i: (i, 0)),
            pl.BlockSpec((tm, MLA_HEADS * V_DIM), lambda i: (i, 0)),
            pl.BlockSpec((tm, MLA_HEADS * QK_PAD), lambda i: (i, 0)),
            pl.BlockSpec((tm, MEM_WIDTH), lambda i: (i, 0)),
        ],
        out_shape=[
            jax.ShapeDtypeStruct((t, MLA_HEADS * QK_PAD), BF16),
            jax.ShapeDtypeStruct((t, MLA_HEADS * V_DIM), BF16),
            jax.ShapeDtypeStruct((t, MLA_HEADS * QK_PAD), BF16),
            jax.ShapeDtypeStruct((t, MEM_WIDTH), BF16),
        ],
        compiler_params=_params(("parallel",)),
        name="b_prep",
    )(h, pos_col, inv_freq, g_kv.reshape(1, D_MODEL), g_q.reshape(1, D_MODEL),
      g_ckv.reshape(1, KV_LORA), g_cq.reshape(1, Q_LORA),
      wkva_bf, wuk_bf, wuv_bf, win_bf, wuq_bf)


def _mla_kernel(q_ref, k_ref, v_ref, o_ref, vaug_scr, *, blk):
    s_len = q_ref.shape[1]
    n_blk = s_len // blk
    neg = jnp.finfo(F32).min
    vaug_scr[:, 0:V_DIM] = v_ref[0]
    lane = lax.broadcasted_iota(jnp.int32, (s_len, LANES), 1)
    vaug_scr[:, V_DIM:] = jnp.where(lane == 0, 1.0, 0.0).astype(BF16)
    row = lax.broadcasted_iota(jnp.int32, (blk, blk), 0)
    col = lax.broadcasted_iota(jnp.int32, (blk, blk), 1)
    causal = row >= col

    def scores(qi, j):
        return _dot_nt(q_ref[0, qi * blk:(qi + 1) * blk, :], k_ref[0, j * blk:(j + 1) * blk, :])

    def accumulate(pending, acc):
        qi, j, p, alpha = pending
        pv = _dot(p, vaug_scr[j * blk:(j + 1) * blk, :])
        acc = pv if j == 0 else alpha * acc + pv
        if j == qi:
            out = acc[:, 0:V_DIM] / acc[:, V_DIM:V_DIM + 1]
            o_ref[0, qi * blk:(qi + 1) * blk, :] = out.astype(BF16)
        return acc

    pairs = [(qi, j) for qi in range(n_blk) for j in range(qi + 1)]
    s_next = scores(*pairs[0])
    m = acc = pending = None
    for t, (qi, j) in enumerate(pairs):
        s = s_next
        if t + 1 < len(pairs):
            s_next = scores(*pairs[t + 1])
        if j == qi:
            s = jnp.where(causal, s, neg)
        s_max = jnp.max(s, axis=-1, keepdims=True)
        if j == 0:
            m_new, alpha = s_max, None
        else:
            m_new = jnp.maximum(m, s_max)
            alpha = jnp.exp2(m - m_new)
        p = jnp.exp2(s - m_new).astype(BF16)
        if pending is not None:
            acc = accumulate(pending, acc)
        pending, m = (qi, j, p, alpha), m_new
    accumulate(pending, acc)


def _mla_attn(q, k, v, blk):
    b, s, _ = q.shape
    kern = functools.partial(_mla_kernel, blk=blk)
    return pl.pallas_call(
        kern,
        grid=(b, MLA_HEADS),
        in_specs=[
            pl.BlockSpec((1, s, QK_PAD), lambda bi, hi: (bi, 0, hi)),
            pl.BlockSpec((1, s, QK_PAD), lambda bi, hi: (bi, 0, hi)),
            pl.BlockSpec((1, s, V_DIM), lambda bi, hi: (bi, 0, hi)),
        ],
        out_specs=pl.BlockSpec((1, s, V_DIM), lambda bi, hi: (bi, 0, hi)),
        out_shape=jax.ShapeDtypeStruct((b, s, MLA_HEADS * V_DIM), BF16),
        scratch_shapes=[pltpu.VMEM((s, 2 * V_DIM), BF16)],
        compiler_params=_params(("parallel", "parallel")),
        name="mla_attn",
    )(q, k, v)


def kernel(x, mem, positions, norm_gains, mem_norm, w_mem_kv, ffn_w_gate, ffn_w_up, ffn_w_down,
           a_w_in, a_ln_g, a_ln_b, a_w_s, a_b_s, a_w_out, kv_src_norm, w_kv_a, kv_norm, w_uk, w_uv,
           b_w_in, b_q_norm, b_w_uq, b_w_out):
    b, s, d = x.shape
    t = b * s
    depth = norm_gains.shape[0]
    n_a = a_w_in.shape[0]
    assert d == D_MODEL and depth == 2 and n_a == 1 and b_w_in.shape[0] == 1

    bf = lambda w: w.astype(BF16)
    wuq = b_w_uq[0].reshape(Q_LORA, MLA_HEADS, QK_NOPE + QK_ROPE)
    wuq = jnp.pad(wuq, ((0, 0), (0, 0), (0, QK_PAD - QK_NOPE - QK_ROPE)))
    wuq_bf = bf(wuq.reshape(Q_LORA, MLA_HEADS * QK_PAD))
    wkva_bf = bf(jnp.pad(w_kv_a, ((0, 0), (0, KVA_PAD - KV_LORA - QK_ROPE))))
    inv_freq = ROPE_THETA ** (-jnp.arange(0, QK_ROPE, 2, dtype=F32) / QK_ROPE)
    inv_freq = jnp.tile(inv_freq, LANES // (QK_ROPE // 2)).reshape(1, LANES)

    h = x.reshape(t, d)
    kvm = _mem_kv(mem.reshape(b * MEM_LEN, d), mem_norm, w_mem_kv)
    kvm = kvm.reshape(depth, b, MEM_LEN, 2 * MEM_WIDTH)

    ffn_w = (ffn_w_gate, ffn_w_up, ffn_w_down)
    tok, qm, w_out0_bf, *ffn0_bf = _a_front(
        h, norm_gains[0, 0], bf(a_w_in[0]), a_ln_g[0], a_ln_b[0], a_w_s[0], a_b_s[0].T,
        cast=[(a_w_out, 0)] + [(w, 0) for w in ffn_w], tm=512)
    h, w_out1_bf, w_in1_bf, wuk_bf, wuv_bf, *ffn1_bf = _tail(
        tok, qm, kvm, h, w_out0_bf, norm_gains[0, 1],
        cast=[(b_w_out, 0), (b_w_in, 0), (w_uk[None], 0), (w_uv[None], 0)] + [(w, 1) for w in ffn_w],
        layer=0, tm=512, seq=s)
    h = _ffn(h, norm_gains[0, 2], *ffn0_bf, norm_gains[0, 3], tm=1024, tk=512)

    k_cat, v_all, q_cat, qm = _b_prep(
        h, positions.reshape(t, 1), inv_freq, kv_src_norm, norm_gains[1, 0], kv_norm, b_q_norm[0],
        wkva_bf, wuk_bf, wuv_bf, w_in1_bf, wuq_bf, tm=512)
    att = _mla_attn(q_cat.reshape(b, s, -1), k_cat.reshape(b, s, -1), v_all.reshape(b, s, -1),
                    blk=512)
    h, = _tail(att.reshape(t, TOK_WIDTH), qm, kvm, h, w_out1_bf, norm_gains[1, 1], cast=[],
               layer=1, tm=512, seq=s)
    h = _ffn(h, norm_gains[1, 2], *ffn1_bf, norm_gains[1, 3], tm=1024, tk=512)
    return h.reshape(b, s, d)
```

```python
import functools

import jax
import jax.numpy as jnp
from jax import lax
from jax.experimental import pallas as pl
from jax.experimental.pallas import tpu as pltpu

D_MODEL = 2048
HEAD_DIM = 128
MEM_LEN = 256
MEM_HEADS = 4
MEM_WIDTH = MEM_HEADS * HEAD_DIM
TOK_WIDTH = D_MODEL - MEM_WIDTH
CHUNK = 128
SGU_GROUPS = TOK_WIDTH // HEAD_DIM
MLA_HEADS = TOK_WIDTH // HEAD_DIM
QK_NOPE = 128
QK_ROPE = 64
V_DIM = 128
Q_LORA = 512
KV_LORA = 512
ROPE_THETA = 10000.0
EPS = 1e-6
MLA_SCALE = (QK_NOPE + QK_ROPE) ** -0.5
MEM_SCALE = HEAD_DIM ** -0.5
LOG2_E = 1.4426950408889634
Q_SCALE = MLA_SCALE * LOG2_E

LANES = 128
QK_PAD = 2 * LANES
KVA_PAD = KV_LORA + LANES
VMEM_LIMIT = 56 * 1024 * 1024
ROW_CHUNK = 128
COL_CHUNK = 256
SUB_ROWS = 256

BF16 = jnp.bfloat16
F32 = jnp.float32


def _dot(a, b):
    return jnp.dot(a, b, preferred_element_type=F32)


def _dot_nt(a, b):
    return lax.dot_general(a, b, (((1,), (1,)), ((), ())), preferred_element_type=F32)


def _rms(x, g):
    return x * lax.rsqrt(jnp.mean(x * x, axis=-1, keepdims=True) + EPS) * g


def _gelu_tanh(x):
    c = 0.7978845608028654
    return x * (0.5 * (1.0 + jnp.tanh(c * (x + 0.044715 * (x * x * x)))))


def _resident(shape):
    nd = len(shape)
    return pl.BlockSpec(shape, lambda *_: (0,) * nd, pipeline_mode=pl.Buffered(1))


def _emit_pipelined(stages):
    pending = None
    for matmul, consume in stages:
        value = matmul()
        if pending is not None:
            pending[1](pending[0])
        pending = (value, consume)
    pending[1](pending[0])


def _interleave(*stage_lists):
    return [stage for group in zip(*stage_lists) for stage in group]


def _params(sem):
    return pltpu.CompilerParams(dimension_semantics=sem, vmem_limit_bytes=VMEM_LIMIT)


def _mem_kv_kernel(mem_ref, g_ref, w_ref, o_ref):
    mn = _rms(mem_ref[...], g_ref[0]).astype(BF16)
    o_ref[0] = _dot(mn, w_ref[0].astype(BF16)).astype(BF16)


def _mem_kv(mem2d, mem_norm, w_mem_kv):
    depth = w_mem_kv.shape[0]
    rows = mem2d.shape[0]
    return pl.pallas_call(
        _mem_kv_kernel,
        grid=(depth,),
        in_specs=[
            pl.BlockSpec((rows, D_MODEL), lambda l: (0, 0)),
            pl.BlockSpec((1, 1, D_MODEL), lambda l: (l, 0, 0)),
            pl.BlockSpec((1, D_MODEL, 2 * MEM_WIDTH), lambda l: (l, 0, 0)),
        ],
        out_specs=pl.BlockSpec((1, rows, 2 * MEM_WIDTH), lambda l: (l, 0, 0)),
        out_shape=jax.ShapeDtypeStruct((depth, rows, 2 * MEM_WIDTH), BF16),
        compiler_params=_params(("arbitrary",)),
        name="mem_kv",
    )(mem2d, mem_norm.reshape(depth, 1, D_MODEL), w_mem_kv)


def _cast_specs(cast, steps):
    in_specs, out_specs, out_shapes = [], [], []
    for w, layer in cast:
        _, rows, cols = w.shape
        slab = rows // steps
        assert slab * steps == rows and slab % 16 == 0
        in_specs.append(pl.BlockSpec((None, slab, cols), lambda i, layer=layer: (layer, i, 0)))
        out_specs.append(pl.BlockSpec((slab, cols), lambda i: (i, 0)))
        out_shapes.append(jax.ShapeDtypeStruct((rows, cols), BF16))
    return in_specs, out_specs, out_shapes


def _cast_slabs(src_refs, dst_refs):
    for src, dst in zip(src_refs, dst_refs):
        dst[...] = src[...].astype(BF16)


def _a_front_kernel(x_ref, g_ref, w_ref, lng_ref, lnb_ref, ws_ref, bst_ref, *rest, n_cast):
    cast_in, (tok_ref, qm_ref), rest = rest[:n_cast], rest[n_cast:n_cast + 2], rest[n_cast + 2:]
    cast_out, (u_scr, vn_scr) = rest[:n_cast], rest[n_cast:]
    _cast_slabs(cast_in, cast_out)
    tm = x_ref.shape[0]
    hn = _rms(x_ref[...], g_ref[...]).astype(BF16)
    u_scr[...] = _gelu_tanh(_dot(hn, w_ref[:, 0:TOK_WIDTH]))
    v = _gelu_tanh(_dot(hn, w_ref[:, TOK_WIDTH:2 * TOK_WIDTH]))
    qm_ref[...] = _dot(hn, w_ref[:, 2 * TOK_WIDTH:]).astype(BF16)
    mu = jnp.mean(v, axis=-1, keepdims=True)
    vc = v - mu
    vn = vc * lax.rsqrt(jnp.mean(vc * vc, axis=-1, keepdims=True) + EPS)
    vn_scr[...] = (vn * lng_ref[...] + lnb_ref[...]).astype(BF16)

    n_chunks = tm // CHUNK
    row = lax.broadcasted_iota(jnp.int32, (CHUNK, CHUNK), 0)
    col = lax.broadcasted_iota(jnp.int32, (CHUNK, CHUNK), 1)
    causal = row >= col
    for g in range(SGU_GROUPS):
        cols = slice(g * HEAD_DIM, (g + 1) * HEAD_DIM)
        wg = jnp.where(causal, ws_ref[g], 0.0).astype(BF16)
        rhs = jnp.concatenate(
            [vn_scr[c * CHUNK:(c + 1) * CHUNK, cols] for c in range(n_chunks)], axis=1)
        s = _dot(wg, rhs) + bst_ref[:, g:g + 1]
        for c in range(n_chunks):
            rows = slice(c * CHUNK, (c + 1) * CHUNK)
            tok_ref[rows, cols] = (u_scr[rows, cols] * s[:, c * CHUNK:(c + 1) * CHUNK]).astype(BF16)


def _a_front(h, gain, w_in_bf, ln_g, ln_b, w_s, b_s_t, cast, tm):
    t = h.shape[0]
    a_in = w_in_bf.shape[1]
    c_in, c_out, c_shapes = _cast_specs(cast, t // tm)
    return pl.pallas_call(
        functools.partial(_a_front_kernel, n_cast=len(cast)),
        grid=(t // tm,),
        in_specs=[
            pl.BlockSpec((tm, D_MODEL), lambda i: (i, 0)),
            _resident((1, D_MODEL)),
            _resident((D_MODEL, a_in)),
            _resident((1, TOK_WIDTH)),
            _resident((1, TOK_WIDTH)),
            _resident((SGU_GROUPS, CHUNK, CHUNK)),
            _resident((CHUNK, SGU_GROUPS)),
        ] + c_in,
        out_specs=[
            pl.BlockSpec((tm, TOK_WIDTH), lambda i: (i, 0)),
            pl.BlockSpec((tm, MEM_WIDTH), lambda i: (i, 0)),
        ] + c_out,
        out_shape=[
            jax.ShapeDtypeStruct((t, TOK_WIDTH), BF16),
            jax.ShapeDtypeStruct((t, MEM_WIDTH), BF16),
        ] + c_shapes,
        scratch_shapes=[
            pltpu.VMEM((tm, TOK_WIDTH), F32),
            pltpu.VMEM((tm, TOK_WIDTH), BF16),
        ],
        compiler_params=_params(("parallel",)),
        name="a_front",
    )(h, gain.reshape(1, D_MODEL), w_in_bf, ln_g.reshape(1, TOK_WIDTH),
      ln_b.reshape(1, TOK_WIDTH), w_s, b_s_t, *[w for w, _ in cast])


def _tail_kernel(p1_ref, qm_ref, kv_ref, h_ref, w_ref, g_ref, *rest, n_cast):
    cast_in, o_ref, cast_out = rest[:n_cast], rest[n_cast], rest[n_cast + 1:]
    _cast_slabs(cast_in, cast_out)
    mix = _dot(p1_ref[...], w_ref[0:TOK_WIDTH, :])
    heads = []
    for hh in range(MEM_HEADS):
        cols = slice(hh * HEAD_DIM, (hh + 1) * HEAD_DIM)
        q = qm_ref[:, cols]
        k = kv_ref[0, :, cols]
        v = kv_ref[0, :, MEM_WIDTH + hh * HEAD_DIM:MEM_WIDTH + (hh + 1) * HEAD_DIM]
        s = _dot_nt(q, k) * MEM_SCALE
        p = jnp.exp(s - jnp.max(s, axis=-1, keepdims=True))
        l = jnp.sum(p, axis=-1, keepdims=True)
        heads.append((_dot(p.astype(BF16), v) / l).astype(BF16))
    mix = mix + _dot(jnp.concatenate(heads, axis=1), w_ref[TOK_WIDTH:, :])
    o_ref[...] = h_ref[...] + _rms(mix, g_ref[...])


def _tail(p1, qm, kvm, h, w_out_bf, gain, cast, layer, tm, seq):
    t = h.shape[0]
    per_batch = seq // tm
    c_in, c_out, c_shapes = _cast_specs(cast, t // tm)
    return pl.pallas_call(
        functools.partial(_tail_kernel, n_cast=len(cast)),
        grid=(t // tm,),
        in_specs=[
            pl.BlockSpec((tm, TOK_WIDTH), lambda i: (i, 0)),
            pl.BlockSpec((tm, MEM_WIDTH), lambda i: (i, 0)),
            pl.BlockSpec((None, 1, MEM_LEN, 2 * MEM_WIDTH),
                         lambda i: (layer, i // per_batch, 0, 0)),
            pl.BlockSpec((tm, D_MODEL), lambda i: (i, 0)),
            _resident((D_MODEL, D_MODEL)),
            _resident((1, D_MODEL)),
        ] + c_in,
        out_specs=[pl.BlockSpec((tm, D_MODEL), lambda i: (i, 0))] + c_out,
        out_shape=[jax.ShapeDtypeStruct((t, D_MODEL), F32)] + c_shapes,
        compiler_params=_params(("parallel",)),
        name="mixer_tail",
    )(p1, qm, kvm, h, w_out_bf, gain.reshape(1, D_MODEL), *[w for w, _ in cast])


def _ffn_kernel(h_ref, gin_ref, wg_ref, wu_ref, wd_ref, gout_ref, o_ref, hn_scr):
    k = pl.program_id(1)
    last = pl.num_programs(1) - 1
    slabs = [slice(lo, lo + ROW_CHUNK) for lo in range(0, h_ref.shape[0], ROW_CHUNK)]

    def step(is_first, is_last):
        if is_first:
            for rows in slabs:
                hn_scr[rows, :] = _rms(h_ref[rows, :], gin_ref[...]).astype(BF16)
        hn = hn_scr[...]
        gate = _dot(hn, wg_ref[...])
        up = _dot(hn, wu_ref[...])
        act = (gate * (1.0 / (1.0 + jnp.exp(-gate))) * up).astype(BF16)
        part = _dot(act, wd_ref[...])
        if is_first:
            o_ref[...] = part
        else:
            o_ref[...] += part
        if is_last:
            for rows in slabs:
                o_ref[rows, :] = h_ref[rows, :] + _rms(o_ref[rows, :], gout_ref[...])

    pl.when(k == 0)(lambda: step(True, False))
    pl.when(jnp.logical_and(k > 0, k < last))(lambda: step(False, False))
    pl.when(k == last)(lambda: step(False, True))


def _ffn(h, g_in, wg_bf, wu_bf, wd_bf, g_out, tm, tk):
    t = h.shape[0]
    d_ff = wg_bf.shape[1]
    assert d_ff // tk >= 2
    return pl.pallas_call(
        _ffn_kernel,
        grid=(t // tm, d_ff // tk),
        in_specs=[
            pl.BlockSpec((tm, D_MODEL), lambda i, k: (i, 0)),
            _resident((1, D_MODEL)),
            pl.BlockSpec((D_MODEL, tk), lambda i, k: (0, k)),
            pl.BlockSpec((D_MODEL, tk), lambda i, k: (0, k)),
            pl.BlockSpec((tk, D_MODEL), lambda i, k: (k, 0)),
            _resident((1, D_MODEL)),
        ],
        out_specs=pl.BlockSpec((tm, D_MODEL), lambda i, k: (i, 0)),
        out_shape=jax.ShapeDtypeStruct((t, D_MODEL), F32),
        scratch_shapes=[pltpu.VMEM((tm, D_MODEL), BF16)],
        compiler_params=_params(("parallel", "arbitrary")),
        name="swiglu_ffn",
    )(h, g_in.reshape(1, D_MODEL), wg_bf, wu_bf, wd_bf, g_out.reshape(1, D_MODEL))


def _rope_group(x, cos, sin_signed, first_half):
    swapped = jnp.where(first_half, pltpu.roll(x, 96, 1), pltpu.roll(x, 32, 1))
    return x * cos + swapped * sin_signed


def _b_prep_kernel(h_ref, pos_ref, invf_ref, gk_ref, gq_ref, gc_ref, gqn_ref,
                   wkva_ref, wuk_ref, wuv_ref, win_ref, wuq_ref,
                   k_ref, v_ref, q_ref, qm_ref):
    subs = [slice(lo, lo + SUB_ROWS) for lo in range(0, h_ref.shape[0], SUB_ROWS)]
    lane = lax.broadcasted_iota(jnp.int32, (SUB_ROWS, LANES), 1)
    first_half = (lane % QK_ROPE) < (QK_ROPE // 2)

    def rope_tables(rows):
        ang = pos_ref[rows, :].astype(F32) * invf_ref[...]
        sin = jnp.sin(ang)
        return jnp.cos(ang), jnp.where(first_half, -sin, sin)

    def sub_block(rows):
        state = {}

        def in_projections():
            x = h_ref[rows, :]
            y = x * lax.rsqrt(jnp.mean(x * x, axis=-1, keepdims=True) + EPS)
            hk = (y * gk_ref[...]).astype(BF16)
            hq = (y * gq_ref[...]).astype(BF16)
            return _dot(hk, wkva_ref[...]), _dot(hq, win_ref[...])

        def latents(az):
            a, z = az
            cos, sin_signed = state["rope"] = rope_tables(rows)
            state["ckv"] = _rms(a[:, 0:KV_LORA], gc_ref[...]).astype(BF16)
            state["k_rope"] = _rope_group(
                a[:, KV_LORA:KVA_PAD], cos, sin_signed, first_half).astype(BF16)
            state["cq"] = _rms(z[:, 0:Q_LORA], gqn_ref[...]).astype(BF16)
            qm_ref[rows, :] = z[:, Q_LORA:].astype(BF16)

        def values():
            return _dot(state["ckv"], wuv_ref[...])

        def store_values(v):
            v_ref[rows, :] = v.astype(BF16)

        def k_chunk(lo):
            def matmul():
                return _dot(state["ckv"], wuk_ref[:, lo:lo + COL_CHUNK])

            def consume(k_nope):
                for c in range(0, COL_CHUNK, QK_NOPE):
                    dst = (lo + c) // QK_NOPE * QK_PAD
                    k_ref[rows, dst:dst + LANES] = k_nope[:, c:c + QK_NOPE].astype(BF16)
                    k_ref[rows, dst + LANES:dst + QK_PAD] = state["k_rope"]
            return matmul, consume

        def q_chunk(lo):
            def matmul():
                return _dot(state["cq"], wuq_ref[:, lo:lo + COL_CHUNK])

            def consume(q):
                cos, sin_signed = state["rope"]
                for c in range(0, COL_CHUNK, QK_PAD):
                    q_ref[rows, lo + c:lo + c + LANES] = (q[:, c:c + LANES] * Q_SCALE).astype(BF16)
                    q_rot = _rope_group(q[:, c + LANES:c + QK_PAD], cos, sin_signed, first_half)
                    q_ref[rows, lo + c + LANES:lo + c + QK_PAD] = (q_rot * Q_SCALE).astype(BF16)
            return matmul, consume

        stages = [(in_projections, latents), (values, store_values)]
        stages += [k_chunk(lo) for lo in range(0, MLA_HEADS * QK_NOPE, COL_CHUNK)]
        stages += [q_chunk(lo) for lo in range(0, MLA_HEADS * QK_PAD, COL_CHUNK)]
        return stages

    assert len(subs) >= 2
    _emit_pipelined(_interleave(*[sub_block(rows) for rows in subs]))


def _b_prep(h, pos_col, inv_freq, g_kv, g_q, g_ckv, g_cq, wkva_bf, wuk_bf, wuv_bf,
            win_bf, wuq_bf, tm):
    t = h.shape[0]
    return pl.pallas_call(
        _b_prep_kernel,
        grid=(t // tm,),
        in_specs=[
            pl.BlockSpec((tm, D_MODEL), lambda i: (i, 0)),
            pl.BlockSpec((tm, 1), lambda i: (i, 0)),
            _resident((1, LANES)),
            _resident((1, D_MODEL)),
            _resident((1, D_MODEL)),
            _resident((1, KV_LORA)),
            _resident((1, Q_LORA)),
            _resident(wkva_bf.shape),
            _resident(wuk_bf.shape),
            _resident(wuv_bf.shape),
            _resident(win_bf.shape),
            _resident(wuq_bf.shape),
        ],
        out_specs=[
            pl.BlockSpec((tm, MLA_HEADS * QK_PAD), lambda i: (i, 0)),
            pl.BlockSpec((tm, MLA_HEADS * V_DIM), lambda i: (i, 0)),
            pl.BlockSpec((tm, MLA_HEADS * QK_PAD), lambda i: (i, 0)),
            pl.BlockSpec((tm, MEM_WIDTH), lambda i: (i, 0)),
        ],
        out_shape=[
            jax.ShapeDtypeStruct((t, MLA_HEADS * QK_PAD), BF16),
            jax.ShapeDtypeStruct((t, MLA_HEADS * V_DIM), BF16),
            jax.ShapeDtypeStruct((t, MLA_HEADS * QK_PAD), BF16),
            jax.ShapeDtypeStruct((t, MEM_WIDTH), BF16),
        ],
        compiler_params=_params(("parallel",)),
        name="b_prep",
    )(h, pos_col, inv_freq, g_kv.reshape(1, D_MODEL), g_q.reshape(1, D_MODEL),
      g_ckv.reshape(1, KV_LORA), g_cq.reshape(1, Q_LORA),
      wkva_bf, wuk_bf, wuv_bf, win_bf, wuq_bf)


def _mla_kernel(q_ref, k_ref, v_ref, o_ref, vaug_scr, *, blk, heads):
    s_len = q_ref.shape[1]
    n_blk = s_len // blk
    neg = jnp.finfo(F32).min
    lane = lax.broadcasted_iota(jnp.int32, (s_len, LANES), 1)
    ones_col = jnp.where(lane == 0, 1.0, 0.0).astype(BF16)
    for hd in range(heads):
        vaug_scr[:, hd * QK_PAD:hd * QK_PAD + V_DIM] = v_ref[0, :, hd * V_DIM:(hd + 1) * V_DIM]
        vaug_scr[:, hd * QK_PAD + V_DIM:(hd + 1) * QK_PAD] = ones_col
    row = lax.broadcasted_iota(jnp.int32, (blk, blk), 0)
    col = lax.broadcasted_iota(jnp.int32, (blk, blk), 1)
    causal = row >= col

    def scores(hd, qi, j):
        cols = slice(hd * QK_PAD, (hd + 1) * QK_PAD)
        return _dot_nt(q_ref[0, qi * blk:(qi + 1) * blk, cols], k_ref[0, j * blk:(j + 1) * blk, cols])

    def accumulate(pending, acc):
        hd, qi, j, p, alpha = pending
        pv = _dot(p, vaug_scr[j * blk:(j + 1) * blk, hd * QK_PAD:(hd + 1) * QK_PAD])
        acc = pv if j == 0 else alpha * acc + pv
        if j == qi:
            out = acc[:, 0:V_DIM] / acc[:, V_DIM:V_DIM + 1]
            o_ref[0, qi * blk:(qi + 1) * blk, hd * V_DIM:(hd + 1) * V_DIM] = out.astype(BF16)
        return acc

    pairs = [(hd, qi, j) for hd in range(heads) for qi in range(n_blk) for j in range(qi + 1)]
    s_next = scores(*pairs[0])
    m = acc = pending = None
    for t, (hd, qi, j) in enumerate(pairs):
        s = s_next
        if t + 1 < len(pairs):
            s_next = scores(*pairs[t + 1])
        if j == qi:
            s = jnp.where(causal, s, neg)
        s_max = jnp.max(s, axis=-1, keepdims=True)
        if j == 0:
            m_new, alpha = s_max, None
        else:
            m_new = jnp.maximum(m, s_max)
            alpha = jnp.exp2(m - m_new)
        p = jnp.exp2(s - m_new).astype(BF16)
        if pending is not None:
            acc = accumulate(pending, acc)
        pending, m = (hd, qi, j, p, alpha), m_new
    accumulate(pending, acc)


def _mla_attn(q, k, v, blk, heads):
    b, s, _ = q.shape
    assert MLA_HEADS % heads == 0
    kern = functools.partial(_mla_kernel, blk=blk, heads=heads)
    return pl.pallas_call(
        kern,
        grid=(b, MLA_HEADS // heads),
        in_specs=[
            pl.BlockSpec((1, s, heads * QK_PAD), lambda bi, hi: (bi, 0, hi)),
            pl.BlockSpec((1, s, heads * QK_PAD), lambda bi, hi: (bi, 0, hi)),
            pl.BlockSpec((1, s, heads * V_DIM), lambda bi, hi: (bi, 0, hi)),
        ],
        out_specs=pl.BlockSpec((1, s, heads * V_DIM), lambda bi, hi: (bi, 0, hi)),
        out_shape=jax.ShapeDtypeStruct((b, s, MLA_HEADS * V_DIM), BF16),
        scratch_shapes=[pltpu.VMEM((s, heads * QK_PAD), BF16)],
        compiler_params=_params(("parallel", "parallel")),
        name="mla_attn",
    )(q, k, v)


def kernel(x, mem, positions, norm_gains, mem_norm, w_mem_kv, ffn_w_gate, ffn_w_up, ffn_w_down,
           a_w_in, a_ln_g, a_ln_b, a_w_s, a_b_s, a_w_out, kv_src_norm, w_kv_a, kv_norm, w_uk, w_uv,
           b_w_in, b_q_norm, b_w_uq, b_w_out):
    b, s, d = x.shape
    t = b * s
    depth = norm_gains.shape[0]
    n_a = a_w_in.shape[0]
    assert d == D_MODEL and depth == 2 and n_a == 1 and b_w_in.shape[0] == 1

    bf = lambda w: w.astype(BF16)
    wuq = b_w_uq[0].reshape(Q_LORA, MLA_HEADS, QK_NOPE + QK_ROPE)
    wuq = jnp.pad(wuq, ((0, 0), (0, 0), (0, QK_PAD - QK_NOPE - QK_ROPE)))
    wuq_bf = bf(wuq.reshape(Q_LORA, MLA_HEADS * QK_PAD))
    wkva_bf = bf(jnp.pad(w_kv_a, ((0, 0), (0, KVA_PAD - KV_LORA - QK_ROPE))))
    inv_freq = ROPE_THETA ** (-jnp.arange(0, QK_ROPE, 2, dtype=F32) / QK_ROPE)
    inv_freq = jnp.tile(inv_freq, LANES // (QK_ROPE // 2)).reshape(1, LANES)

    h = x.reshape(t, d)
    kvm = _mem_kv(mem.reshape(b * MEM_LEN, d), mem_norm, w_mem_kv)
    kvm = kvm.reshape(depth, b, MEM_LEN, 2 * MEM_WIDTH)

    ffn_w = (ffn_w_gate, ffn_w_up, ffn_w_down)
    tok, qm, w_out0_bf, *ffn0_bf = _a_front(
        h, norm_gains[0, 0], bf(a_w_in[0]), a_ln_g[0], a_ln_b[0], a_w_s[0], a_b_s[0].T,
        cast=[(a_w_out, 0)] + [(w, 0) for w in ffn_w], tm=512)
    h, w_out1_bf, w_in1_bf, wuk_bf, wuv_bf, *ffn1_bf = _tail(
        tok, qm, kvm, h, w_out0_bf, norm_gains[0, 1],
        cast=[(b_w_out, 0), (b_w_in, 0), (w_uk[None], 0), (w_uv[None], 0)] + [(w, 1) for w in ffn_w],
        layer=0, tm=512, seq=s)
    h = _ffn(h, norm_gains[0, 2], *ffn0_bf, norm_gains[0, 3], tm=1024, tk=512)

    k_cat, v_all, q_cat, qm = _b_prep(
        h, positions.reshape(t, 1), inv_freq, kv_src_norm, norm_gains[1, 0], kv_norm, b_q_norm[0],
        wkva_bf, wuk_bf, wuv_bf, w_in1_bf, wuq_bf, tm=512)
    att = _mla_attn(q_cat.reshape(b, s, -1), k_cat.reshape(b, s, -1), v_all.reshape(b, s, -1),
                    blk=512, heads=2)
    h, = _tail(att.reshape(t, TOK_WIDTH), qm, kvm, h, w_out1_bf, norm_gains[1, 1], cast=[],
               layer=1, tm=512, seq=s)
    h = _ffn(h, norm_gains[1, 2], *ffn1_bf, norm_gains[1, 3], tm=1024, tk=512)
    return h.reshape(b, s, d)
```

```python
import functools

import jax
import jax.numpy as jnp
from jax import lax
from jax.experimental import pallas as pl
from jax.experimental.pallas import tpu as pltpu

D_MODEL = 2048
HEAD_DIM = 128
MEM_LEN = 256
MEM_HEADS = 4
MEM_WIDTH = MEM_HEADS * HEAD_DIM
TOK_WIDTH = D_MODEL - MEM_WIDTH
CHUNK = 128
SGU_GROUPS = TOK_WIDTH // HEAD_DIM
MLA_HEADS = TOK_WIDTH // HEAD_DIM
QK_NOPE = 128
QK_ROPE = 64
V_DIM = 128
Q_LORA = 512
KV_LORA = 512
ROPE_THETA = 10000.0
EPS = 1e-6
MLA_SCALE = (QK_NOPE + QK_ROPE) ** -0.5
MEM_SCALE = HEAD_DIM ** -0.5
LOG2_E = 1.4426950408889634
Q_SCALE = MLA_SCALE * LOG2_E

LANES = 128
QK_PAD = 2 * LANES
KVA_PAD = KV_LORA + LANES
VMEM_LIMIT = 56 * 1024 * 1024
ROW_CHUNK = 128
COL_CHUNK = 256
SUB_ROWS = 256

BF16 = jnp.bfloat16
F32 = jnp.float32


def _dot(a, b):
    return jnp.dot(a, b, preferred_element_type=F32)


def _dot_nt(a, b):
    return lax.dot_general(a, b, (((1,), (1,)), ((), ())), preferred_element_type=F32)


def _rms(x, g):
    return x * lax.rsqrt(jnp.mean(x * x, axis=-1, keepdims=True) + EPS) * g


def _gelu_tanh(x):
    c = 0.7978845608028654
    return x * (0.5 * (1.0 + jnp.tanh(c * (x + 0.044715 * (x * x * x)))))


def _resident(shape):
    nd = len(shape)
    return pl.BlockSpec(shape, lambda *_: (0,) * nd, pipeline_mode=pl.Buffered(1))


def _emit_pipelined(stages):
    pending = None
    for matmul, consume in stages:
        value = matmul()
        if pending is not None:
            pending[1](pending[0])
        pending = (value, consume)
    pending[1](pending[0])


def _interleave(*stage_lists):
    return [stage for group in zip(*stage_lists) for stage in group]


def _params(sem):
    return pltpu.CompilerParams(dimension_semantics=sem, vmem_limit_bytes=VMEM_LIMIT)


def _mem_kv_kernel(mem_ref, g_ref, w_ref, o_ref):
    mn = _rms(mem_ref[...], g_ref[0]).astype(BF16)
    o_ref[0] = _dot(mn, w_ref[0].astype(BF16)).astype(BF16)


def _mem_kv(mem2d, mem_norm, w_mem_kv):
    depth = w_mem_kv.shape[0]
    rows = mem2d.shape[0]
    return pl.pallas_call(
        _mem_kv_kernel,
        grid=(depth,),
        in_specs=[
            pl.BlockSpec((rows, D_MODEL), lambda l: (0, 0)),
            pl.BlockSpec((1, 1, D_MODEL), lambda l: (l, 0, 0)),
            pl.BlockSpec((1, D_MODEL, 2 * MEM_WIDTH), lambda l: (l, 0, 0)),
        ],
        out_specs=pl.BlockSpec((1, rows, 2 * MEM_WIDTH), lambda l: (l, 0, 0)),
        out_shape=jax.ShapeDtypeStruct((depth, rows, 2 * MEM_WIDTH), BF16),
        compiler_params=_params(("arbitrary",)),
        name="mem_kv",
    )(mem2d, mem_norm.reshape(depth, 1, D_MODEL), w_mem_kv)


def _cast_specs(cast, steps):
    in_specs, out_specs, out_shapes = [], [], []
    for w, layer in cast:
        _, rows, cols = w.shape
        slab = rows // steps
        assert slab * steps == rows and slab % 16 == 0
        in_specs.append(pl.BlockSpec((None, slab, cols), lambda i, layer=layer: (layer, i, 0)))
        out_specs.append(pl.BlockSpec((slab, cols), lambda i: (i, 0)))
        out_shapes.append(jax.ShapeDtypeStruct((rows, cols), BF16))
    return in_specs, out_specs, out_shapes


def _cast_slabs(src_refs, dst_refs):
    for src, dst in zip(src_refs, dst_refs):
        dst[...] = src[...].astype(BF16)


def _a_front_kernel(x_ref, g_ref, w_ref, lng_ref, lnb_ref, ws_ref, bst_ref, *rest, n_cast):
    cast_in, (tok_ref, qm_ref), rest = rest[:n_cast], rest[n_cast:n_cast + 2], rest[n_cast + 2:]
    cast_out, (u_scr, vn_scr) = rest[:n_cast], rest[n_cast:]
    _cast_slabs(cast_in, cast_out)
    tm = x_ref.shape[0]
    hn = _rms(x_ref[...], g_ref[...]).astype(BF16)
    u_scr[...] = _gelu_tanh(_dot(hn, w_ref[:, 0:TOK_WIDTH]))
    v = _gelu_tanh(_dot(hn, w_ref[:, TOK_WIDTH:2 * TOK_WIDTH]))
    qm_ref[...] = _dot(hn, w_ref[:, 2 * TOK_WIDTH:]).astype(BF16)
    mu = jnp.mean(v, axis=-1, keepdims=True)
    vc = v - mu
    vn = vc * lax.rsqrt(jnp.mean(vc * vc, axis=-1, keepdims=True) + EPS)
    vn_scr[...] = (vn * lng_ref[...] + lnb_ref[...]).astype(BF16)

    n_chunks = tm // CHUNK
    row = lax.broadcasted_iota(jnp.int32, (CHUNK, CHUNK), 0)
    col = lax.broadcasted_iota(jnp.int32, (CHUNK, CHUNK), 1)
    causal = row >= col
    for g in range(SGU_GROUPS):
        cols = slice(g * HEAD_DIM, (g + 1) * HEAD_DIM)
        wg = jnp.where(causal, ws_ref[g], 0.0).astype(BF16)
        rhs = jnp.concatenate(
            [vn_scr[c * CHUNK:(c + 1) * CHUNK, cols] for c in range(n_chunks)], axis=1)
        s = _dot(wg, rhs) + bst_ref[:, g:g + 1]
        for c in range(n_chunks):
            rows = slice(c * CHUNK, (c + 1) * CHUNK)
            tok_ref[rows, cols] = (u_scr[rows, cols] * s[:, c * CHUNK:(c + 1) * CHUNK]).astype(BF16)


def _a_front(h, gain, w_in_bf, ln_g, ln_b, w_s, b_s_t, cast, tm):
    t = h.shape[0]
    a_in = w_in_bf.shape[1]
    c_in, c_out, c_shapes = _cast_specs(cast, t // tm)
    return pl.pallas_call(
        functools.partial(_a_front_kernel, n_cast=len(cast)),
        grid=(t // tm,),
        in_specs=[
            pl.BlockSpec((tm, D_MODEL), lambda i: (i, 0)),
            _resident((1, D_MODEL)),
            _resident((D_MODEL, a_in)),
            _resident((1, TOK_WIDTH)),
            _resident((1, TOK_WIDTH)),
            _resident((SGU_GROUPS, CHUNK, CHUNK)),
            _resident((CHUNK, SGU_GROUPS)),
        ] + c_in,
        out_specs=[
            pl.BlockSpec((tm, TOK_WIDTH), lambda i: (i, 0)),
            pl.BlockSpec((tm, MEM_WIDTH), lambda i: (i, 0)),
        ] + c_out,
        out_shape=[
            jax.ShapeDtypeStruct((t, TOK_WIDTH), BF16),
            jax.ShapeDtypeStruct((t, MEM_WIDTH), BF16),
        ] + c_shapes,
        scratch_shapes=[
            pltpu.VMEM((tm, TOK_WIDTH), F32),
            pltpu.VMEM((tm, TOK_WIDTH), BF16),
        ],
        compiler_params=_params(("parallel",)),
        name="a_front",
    )(h, gain.reshape(1, D_MODEL), w_in_bf, ln_g.reshape(1, TOK_WIDTH),
      ln_b.reshape(1, TOK_WIDTH), w_s, b_s_t, *[w for w, _ in cast])


def _tail_kernel(p1_ref, qm_ref, kv_ref, h_ref, w_ref, g_ref, *rest, n_cast):
    cast_in, o_ref, cast_out = rest[:n_cast], rest[n_cast], rest[n_cast + 1:]
    _cast_slabs(cast_in, cast_out)
    mix = _dot(p1_ref[...], w_ref[0:TOK_WIDTH, :])
    heads = []
    for hh in range(MEM_HEADS):
        cols = slice(hh * HEAD_DIM, (hh + 1) * HEAD_DIM)
        q = qm_ref[:, cols]
        k = kv_ref[0, :, cols]
        v = kv_ref[0, :, MEM_WIDTH + hh * HEAD_DIM:MEM_WIDTH + (hh + 1) * HEAD_DIM]
        s = _dot_nt(q, k) * MEM_SCALE
        p = jnp.exp(s - jnp.max(s, axis=-1, keepdims=True))
        l = jnp.sum(p, axis=-1, keepdims=True)
        heads.append((_dot(p.astype(BF16), v) / l).astype(BF16))
    mix = mix + _dot(jnp.concatenate(heads, axis=1), w_ref[TOK_WIDTH:, :])
    o_ref[...] = h_ref[...] + _rms(mix, g_ref[...])


def _tail(p1, qm, kvm, h, w_out_bf, gain, cast, layer, tm, seq):
    t = h.shape[0]
    per_batch = seq // tm
    c_in, c_out, c_shapes = _cast_specs(cast, t // tm)
    return pl.pallas_call(
        functools.partial(_tail_kernel, n_cast=len(cast)),
        grid=(t // tm,),
        in_specs=[
            pl.BlockSpec((tm, TOK_WIDTH), lambda i: (i, 0)),
            pl.BlockSpec((tm, MEM_WIDTH), lambda i: (i, 0)),
            pl.BlockSpec((None, 1, MEM_LEN, 2 * MEM_WIDTH),
                         lambda i: (layer, i // per_batch, 0, 0)),
            pl.BlockSpec((tm, D_MODEL), lambda i: (i, 0)),
            _resident((D_MODEL, D_MODEL)),
            _resident((1, D_MODEL)),
        ] + c_in,
        out_specs=[pl.BlockSpec((tm, D_MODEL), lambda i: (i, 0))] + c_out,
        out_shape=[jax.ShapeDtypeStruct((t, D_MODEL), F32)] + c_shapes,
        compiler_params=_params(("parallel",)),
        name="mixer_tail",
    )(p1, qm, kvm, h, w_out_bf, gain.reshape(1, D_MODEL), *[w for w, _ in cast])


def _ffn_kernel(h_ref, gin_ref, wg_ref, wu_ref, wd_ref, gout_ref, o_ref, hn_scr):
    k = pl.program_id(1)
    last = pl.num_programs(1) - 1
    slabs = [slice(lo, lo + ROW_CHUNK) for lo in range(0, h_ref.shape[0], ROW_CHUNK)]

    def step(is_first, is_last):
        if is_first:
            for rows in slabs:
                hn_scr[rows, :] = _rms(h_ref[rows, :], gin_ref[...]).astype(BF16)
        hn = hn_scr[...]
        gate = _dot(hn, wg_ref[...])
        up = _dot(hn, wu_ref[...])
        act = (gate * (1.0 / (1.0 + jnp.exp(-gate))) * up).astype(BF16)
        part = _dot(act, wd_ref[...])
        if is_first:
            o_ref[...] = part
        else:
            o_ref[...] += part
        if is_last:
            for rows in slabs:
                o_ref[rows, :] = h_ref[rows, :] + _rms(o_ref[rows, :], gout_ref[...])

    pl.when(k == 0)(lambda: step(True, False))
    pl.when(jnp.logical_and(k > 0, k < last))(lambda: step(False, False))
    pl.when(k == last)(lambda: step(False, True))


def _ffn(h, g_in, wg_bf, wu_bf, wd_bf, g_out, tm, tk):
    t = h.shape[0]
    d_ff = wg_bf.shape[1]
    assert d_ff // tk >= 2
    return pl.pallas_call(
        _ffn_kernel,
        grid=(t // tm, d_ff // tk),
        in_specs=[
            pl.BlockSpec((tm, D_MODEL), lambda i, k: (i, 0)),
            _resident((1, D_MODEL)),
            pl.BlockSpec((D_MODEL, tk), lambda i, k: (0, k)),
            pl.BlockSpec((D_MODEL, tk), lambda i, k: (0, k)),
            pl.BlockSpec((tk, D_MODEL), lambda i, k: (k, 0)),
            _resident((1, D_MODEL)),
        ],
        out_specs=pl.BlockSpec((tm, D_MODEL), lambda i, k: (i, 0)),
        out_shape=jax.ShapeDtypeStruct((t, D_MODEL), F32),
        scratch_shapes=[pltpu.VMEM((tm, D_MODEL), BF16)],
        compiler_params=_params(("parallel", "arbitrary")),
        name="swiglu_ffn",
    )(h, g_in.reshape(1, D_MODEL), wg_bf, wu_bf, wd_bf, g_out.reshape(1, D_MODEL))


def _rope_group(x, cos, sin_signed, first_half):
    swapped = jnp.where(first_half, pltpu.roll(x, 96, 1), pltpu.roll(x, 32, 1))
    return x * cos + swapped * sin_signed


def _b_prep_kernel(h_ref, pos_ref, invf_ref, gk_ref, gq_ref, gc_ref, gqn_ref,
                   wkva_ref, wuk_ref, wuv_ref, win_ref, wuq_ref,
                   k_ref, v_ref, q_ref, qm_ref):
    subs = [slice(lo, lo + SUB_ROWS) for lo in range(0, h_ref.shape[0], SUB_ROWS)]
    lane = lax.broadcasted_iota(jnp.int32, (SUB_ROWS, LANES), 1)
    first_half = (lane % QK_ROPE) < (QK_ROPE // 2)

    def rope_tables(rows):
        ang = pos_ref[rows, :].astype(F32) * invf_ref[...]
        sin = jnp.sin(ang)
        return jnp.cos(ang), jnp.where(first_half, -sin, sin)

    def sub_block(rows):
        state = {}

        def in_projections():
            x = h_ref[rows, :]
            y = x * lax.rsqrt(jnp.mean(x * x, axis=-1, keepdims=True) + EPS)
            hk = (y * gk_ref[...]).astype(BF16)
            hq = (y * gq_ref[...]).astype(BF16)
            return _dot(hk, wkva_ref[...]), _dot(hq, win_ref[...])

        def latents(az):
            a, z = az
            cos, sin_signed = state["rope"] = rope_tables(rows)
            state["ckv"] = _rms(a[:, 0:KV_LORA], gc_ref[...]).astype(BF16)
            state["k_rope"] = _rope_group(
                a[:, KV_LORA:KVA_PAD], cos, sin_signed, first_half).astype(BF16)
            state["cq"] = _rms(z[:, 0:Q_LORA], gqn_ref[...]).astype(BF16)
            qm_ref[rows, :] = z[:, Q_LORA:].astype(BF16)

        def values():
            return _dot(state["ckv"], wuv_ref[...])

        def store_values(v):
            v_ref[rows, :] = v.astype(BF16)

        def k_chunk(lo):
            def matmul():
                return _dot(state["ckv"], wuk_ref[:, lo:lo + COL_CHUNK])

            def consume(k_nope):
                for c in range(0, COL_CHUNK, QK_NOPE):
                    dst = (lo + c) // QK_NOPE * QK_PAD
                    k_ref[rows, dst:dst + LANES] = k_nope[:, c:c + QK_NOPE].astype(BF16)
                    k_ref[rows, dst + LANES:dst + QK_PAD] = state["k_rope"]
            return matmul, consume

        def q_chunk(lo):
            def matmul():
                return _dot(state["cq"], wuq_ref[:, lo:lo + COL_CHUNK])

            def consume(q):
                cos, sin_signed = state["rope"]
                for c in range(0, COL_CHUNK, QK_PAD):
                    q_ref[rows, lo + c:lo + c + LANES] = (q[:, c:c + LANES] * Q_SCALE).astype(BF16)
                    q_rot = _rope_group(q[:, c + LANES:c + QK_PAD], cos, sin_signed, first_half)
                    q_ref[rows, lo + c + LANES:lo + c + QK_PAD] = (q_rot * Q_SCALE).astype(BF16)
            return matmul, consume

        stages = [(in_projections, latents), (values, store_values)]
        stages += [k_chunk(lo) for lo in range(0, MLA_HEADS * QK_NOPE, COL_CHUNK)]
        stages += [q_chunk(lo) for lo in range(0, MLA_HEADS * QK_PAD, COL_CHUNK)]
        return stages

    assert len(subs) >= 2
    _emit_pipelined(_interleave(*[sub_block(rows) for rows in subs]))


def _b_prep(h, pos_col, inv_freq, g_kv, g_q, g_ckv, g_cq, wkva_bf, wuk_bf, wuv_bf,
            win_bf, wuq_bf, tm):
    t = h.shape[0]
    return pl.pallas_call(
        _b_prep_kernel,
        grid=(t // tm,),
        in_specs=[
            pl.BlockSpec((tm, D_MODEL), lambda i: (i, 0)),
            pl.BlockSpec((tm, 1), lambda i: (i, 0)),
            _resident((1, LANES)),
            _resident((1, D_MODEL)),
            _resident((1, D_MODEL)),
            _resident((1, KV_LORA)),
            _resident((1, Q_LORA)),
            _resident(wkva_bf.shape),
            _resident(wuk_bf.shape),
            _resident(wuv_bf.shape),
            _resident(win_bf.shape),
            _resident(wuq_bf.shape),
        ],
        out_specs=[
            pl.BlockSpec((tm, MLA_HEADS * QK_PAD), lambda i: (i, 0)),
            pl.BlockSpec((tm, MLA_HEADS * V_DIM), lambda i: (i, 0)),
            pl.BlockSpec((tm, MLA_HEADS * QK_PAD), lambda i: (i, 0)),
            pl.BlockSpec((tm, MEM_WIDTH), lambda i: (i, 0)),
        ],
        out_shape=[
            jax.ShapeDtypeStruct((t, MLA_HEADS * QK_PAD), BF16),
            jax.ShapeDtypeStruct((t, MLA_HEADS * V_DIM), BF16),
            jax.ShapeDtypeStruct((t, MLA_HEADS * QK_PAD), BF16),
            jax.ShapeDtypeStruct((t, MEM_WIDTH), BF16),
        ],
        compiler_params=_params(("parallel",)),
        name="b_prep",
    )(h, pos_col, inv_freq, g_kv.reshape(1, D_MODEL), g_q.reshape(1, D_MODEL),
      g_ckv.reshape(1, KV_LORA), g_cq.reshape(1, Q_LORA),
      wkva_bf, wuk_bf, wuv_bf, win_bf, wuq_bf)


def _mla_kernel(q_ref, k_ref, v_ref, o_ref, vaug_scr, *, blk, heads):
    s_len = q_ref.shape[1]
    n_blk = s_len // blk
    neg = jnp.finfo(F32).min
    lane = lax.broadcasted_iota(jnp.int32, (s_len, LANES), 1)
    ones_col = jnp.where(lane == 0, 1.0, 0.0).astype(BF16)
    for hd in range(heads):
        vaug_scr[:, hd * QK_PAD:hd * QK_PAD + V_DIM] = v_ref[0, :, hd * V_DIM:(hd + 1) * V_DIM]
        vaug_scr[:, hd * QK_PAD + V_DIM:(hd + 1) * QK_PAD] = ones_col
    half = blk // 2
    row_t = lax.broadcasted_iota(jnp.int32, (half, half), 0)
    col_t = lax.broadcasted_iota(jnp.int32, (half, half), 1)
    row_b = lax.broadcasted_iota(jnp.int32, (half, blk), 0) + half
    col_b = lax.broadcasted_iota(jnp.int32, (half, blk), 1)
    causal_parts = (row_t >= col_t, row_b >= col_b)
    kv_len = (half, blk)

    def scores(hd, qi, j):
        cols = slice(hd * QK_PAD, (hd + 1) * QK_PAD)
        if j < qi:
            return _dot_nt(q_ref[0, qi * blk:(qi + 1) * blk, cols], k_ref[0, j * blk:(j + 1) * blk, cols])
        return tuple(
            _dot_nt(q_ref[0, qi * blk + part * half:qi * blk + (part + 1) * half, cols],
                    k_ref[0, j * blk:j * blk + kv_len[part], cols]) for part in range(2))

    def softmax(s, m):
        s_max = jnp.max(s, axis=-1, keepdims=True)
        if m is None:
            m_new, alpha = s_max, None
        else:
            m_new = jnp.maximum(m, s_max)
            alpha = jnp.exp2(m - m_new)
        return jnp.exp2(s - m_new).astype(BF16), alpha, m_new

    def update(hd, j, n_kv, p, alpha, acc):
        pv = _dot(p, vaug_scr[j * blk:j * blk + n_kv, hd * QK_PAD:(hd + 1) * QK_PAD])
        return pv if alpha is None else alpha * acc + pv

    def accumulate(pending, acc):
        hd, qi, j, p, alpha = pending
        if j < qi:
            return update(hd, j, blk, p, alpha, acc)
        for part in range(2):
            rows = slice(part * half, (part + 1) * half)
            acc_part = update(hd, j, kv_len[part], p[part], None if alpha is None else alpha[part],
                              None if acc is None else acc[rows])
            out = acc_part[:, 0:V_DIM] / acc_part[:, V_DIM:V_DIM + 1]
            o_ref[0, qi * blk + part * half:qi * blk + (part + 1) * half,
                  hd * V_DIM:(hd + 1) * V_DIM] = out.astype(BF16)
        return None

    pairs = [(hd, qi, j) for hd in range(heads) for qi in range(n_blk) for j in range(qi + 1)]
    s_next = scores(*pairs[0])
    m = acc = pending = None
    for t, (hd, qi, j) in enumerate(pairs):
        s = s_next
        if t + 1 < len(pairs):
            s_next = scores(*pairs[t + 1])
        if j < qi:
            p, alpha, m_new = softmax(s, m)
        else:
            parts = [softmax(jnp.where(causal_parts[part], s[part], neg),
                             None if m is None else m[part * half:(part + 1) * half])
                     for part in range(2)]
            p = tuple(part[0] for part in parts)
            alpha = None if m is None else tuple(part[1] for part in parts)
            m_new = None
        if pending is not None:
            acc = accumulate(pending, acc)
        pending, m = (hd, qi, j, p, alpha), m_new
    accumulate(pending, acc)


def _mla_attn(q, k, v, blk, heads):
    b, s, _ = q.shape
    assert MLA_HEADS % heads == 0
    kern = functools.partial(_mla_kernel, blk=blk, heads=heads)
    return pl.pallas_call(
        kern,
        grid=(b, MLA_HEADS // heads),
        in_specs=[
            pl.BlockSpec((1, s, heads * QK_PAD), lambda bi, hi: (bi, 0, hi)),
            pl.BlockSpec((1, s, heads * QK_PAD), lambda bi, hi: (bi, 0, hi)),
            pl.BlockSpec((1, s, heads * V_DIM), lambda bi, hi: (bi, 0, hi)),
        ],
        out_specs=pl.BlockSpec((1, s, heads * V_DIM), lambda bi, hi: (bi, 0, hi)),
        out_shape=jax.ShapeDtypeStruct((b, s, MLA_HEADS * V_DIM), BF16),
        scratch_shapes=[pltpu.VMEM((s, heads * QK_PAD), BF16)],
        compiler_params=_params(("parallel", "parallel")),
        name="mla_attn",
    )(q, k, v)


def kernel(x, mem, positions, norm_gains, mem_norm, w_mem_kv, ffn_w_gate, ffn_w_up, ffn_w_down,
           a_w_in, a_ln_g, a_ln_b, a_w_s, a_b_s, a_w_out, kv_src_norm, w_kv_a, kv_norm, w_uk, w_uv,
           b_w_in, b_q_norm, b_w_uq, b_w_out):
    b, s, d = x.shape
    t = b * s
    depth = norm_gains.shape[0]
    n_a = a_w_in.shape[0]
    assert d == D_MODEL and depth == 2 and n_a == 1 and b_w_in.shape[0] == 1

    bf = lambda w: w.astype(BF16)
    wuq = b_w_uq[0].reshape(Q_LORA, MLA_HEADS, QK_NOPE + QK_ROPE)
    wuq = jnp.pad(wuq, ((0, 0), (0, 0), (0, QK_PAD - QK_NOPE - QK_ROPE)))
    wuq_bf = bf(wuq.reshape(Q_LORA, MLA_HEADS * QK_PAD))
    wkva_bf = bf(jnp.pad(w_kv_a, ((0, 0), (0, KVA_PAD - KV_LORA - QK_ROPE))))
    inv_freq = ROPE_THETA ** (-jnp.arange(0, QK_ROPE, 2, dtype=F32) / QK_ROPE)
    inv_freq = jnp.tile(inv_freq, LANES // (QK_ROPE // 2)).reshape(1, LANES)

    h = x.reshape(t, d)
    kvm = _mem_kv(mem.reshape(b * MEM_LEN, d), mem_norm, w_mem_kv)
    kvm = kvm.reshape(depth, b, MEM_LEN, 2 * MEM_WIDTH)

    ffn_w = (ffn_w_gate, ffn_w_up, ffn_w_down)
    tok, qm, w_out0_bf, *ffn0_bf = _a_front(
        h, norm_gains[0, 0], bf(a_w_in[0]), a_ln_g[0], a_ln_b[0], a_w_s[0], a_b_s[0].T,
        cast=[(a_w_out, 0)] + [(w, 0) for w in ffn_w], tm=512)
    h, w_out1_bf, w_in1_bf, wuk_bf, wuv_bf, *ffn1_bf = _tail(
        tok, qm, kvm, h, w_out0_bf, norm_gains[0, 1],
        cast=[(b_w_out, 0), (b_w_in, 0), (w_uk[None], 0), (w_uv[None], 0)] + [(w, 1) for w in ffn_w],
        layer=0, tm=512, seq=s)
    h = _ffn(h, norm_gains[0, 2], *ffn0_bf, norm_gains[0, 3], tm=1024, tk=512)

    k_cat, v_all, q_cat, qm = _b_prep(
        h, positions.reshape(t, 1), inv_freq, kv_src_norm, norm_gains[1, 0], kv_norm, b_q_norm[0],
        wkva_bf, wuk_bf, wuv_bf, w_in1_bf, wuq_bf, tm=512)
    att = _mla_attn(q_cat.reshape(b, s, -1), k_cat.reshape(b, s, -1), v_all.reshape(b, s, -1),
                    blk=512, heads=2)
    h, = _tail(att.reshape(t, TOK_WIDTH), qm, kvm, h, w_out1_bf, norm_gains[1, 1], cast=[],
               layer=1, tm=512, seq=s)
    h = _ffn(h, norm_gains[1, 2], *ffn1_bf, norm_gains[1, 3], tm=1024, tk=512)
    return h.reshape(b, s, d)
```
